```python
import jax, jax.numpy as jnp
from jax import lax
import numpy as np

D_MODEL = 2048
BATCH = 4
SEQ = 4096
DEPTH = 2

CTX_LEN = 256
GRID_W = 64
ROPE_BASE = 10000.0
Q_BLOCK = 128
TOKEN_BLOCK = 128
NORM_EPS = 1e-5
RMS_EPS = 1e-6

MLA_HEADS = 8
MLA_Q_LORA = 512
MLA_KV_LORA = 256
MLA_NOPE = 64
MLA_ROPE = 32
MLA_V = 64
MLA_SCALE = (MLA_NOPE + MLA_ROPE) ** -0.5

RET_HEADS = 8
RET_DK = 32
RET_DV = 64
RET_CHUNK = 128
RET_K_SCALE = RET_DK ** -0.5

GQA_HEADS = 8
GQA_KV_HEADS = 2
GQA_GROUP = GQA_HEADS // GQA_KV_HEADS
GQA_DH = 128
GQA_SCALE = GQA_DH ** -0.5

PEER_HEADS = 8
PEER_DQ = 128
PEER_N_KEYS = 128
PEER_TOPK = 16
PEER_N_EXPERTS = PEER_N_KEYS * PEER_N_KEYS

IN_SPLITS = (MLA_Q_LORA, MLA_KV_LORA, MLA_ROPE,
             RET_HEADS * RET_DK, RET_HEADS * RET_DK, RET_HEADS * RET_DV, RET_HEADS * RET_DV,
             GQA_HEADS * GQA_DH, GQA_KV_HEADS * GQA_DH, GQA_KV_HEADS * GQA_DH,
             D_MODEL, D_MODEL, D_MODEL)
N_IN = sum(IN_SPLITS)

DEEPNORM_ALPHA = (2.0 * DEPTH) ** 0.25
DEEPNORM_BETA = (8.0 * DEPTH) ** -0.25

kernel_name = 'hybrid_mla_retention_gqa_peer_dit_block'


def split_in(p):
    idx = np.cumsum(IN_SPLITS)[:-1].tolist()
    return jnp.split(p, idx, axis=-1)


def layer_norm(x, g, b):
    xf = x.astype(jnp.float32)
    mu = xf.mean(-1, keepdims=True)
    var = jnp.square(xf - mu).mean(-1, keepdims=True)
    return ((xf - mu) * lax.rsqrt(var + NORM_EPS) * g + b).astype(x.dtype)


def rms_norm(x, g):
    xf = x.astype(jnp.float32)
    return (xf * lax.rsqrt(jnp.square(xf).mean(-1, keepdims=True) + RMS_EPS) * g).astype(x.dtype)


def axial_rope(x, row, col):
    r = x.shape[-1]
    ra = r // 2
    nf = ra // 2
    inv_freq = ROPE_BASE ** (-jnp.arange(nf, dtype=jnp.float32) / nf)

    def rotate(xa, pos):
        ang = pos.astype(jnp.float32)[:, None] * inv_freq[None, :]
        cos = jnp.cos(ang)[None, :, None, :]
        sin = jnp.sin(ang)[None, :, None, :]
        x1 = xa[..., :nf].astype(jnp.float32)
        x2 = xa[..., nf:].astype(jnp.float32)
        return jnp.concatenate([x1 * cos - x2 * sin, x1 * sin + x2 * cos], axis=-1)

    out = jnp.concatenate([rotate(x[..., :ra], row), rotate(x[..., ra:], col)], axis=-1)
    return out.astype(x.dtype)


def block_attention(q, k, v, scale):
    B, S, HK, G, DQ = q.shape
    nb = S // Q_BLOCK
    qb = jnp.moveaxis(q.reshape(B, nb, Q_BLOCK, HK, G, DQ), 1, 0)

    def one_block(qx):
        s = jnp.einsum('bqhgd,bthd->bhgqt', qx, k).astype(jnp.float32) * scale
        pr = jax.nn.softmax(s, axis=-1).astype(v.dtype)
        return jnp.einsum('bhgqt,bthe->bqhge', pr, v)

    o = lax.map(one_block, qb)
    return jnp.moveaxis(o, 0, 1).reshape(B, S, HK * G * v.shape[-1])


def mla_q(cq, p, pos):
    B, T, _ = cq.shape
    q = (rms_norm(cq, p['mla_q_norm']) @ p['mla_w_qup']).reshape(B, T, MLA_HEADS, MLA_NOPE + MLA_ROPE)
    if pos is not None:
        q = jnp.concatenate([q[..., :MLA_NOPE], axial_rope(q[..., MLA_NOPE:], *pos)], axis=-1)
    return q[:, :, :, None, :]


def mla_kv(ckv, kr, p, pos):
    B, T, _ = ckv.shape
    kv = (rms_norm(ckv, p['mla_kv_norm']) @ p['mla_w_kvup']).reshape(B, T, MLA_HEADS, MLA_NOPE + MLA_V)
    k_rope = kr[:, :, None, :]
    if pos is not None:
        k_rope = axial_rope(k_rope, *pos)
    k = jnp.concatenate([kv[..., :MLA_NOPE], jnp.broadcast_to(k_rope, (B, T, MLA_HEADS, MLA_ROPE))], axis=-1)
    return k, kv[..., MLA_NOPE:]


def ret_heads(a, d, pos):
    B, T, _ = a.shape
    a = a.reshape(B, T, RET_HEADS, d)
    if pos is not None:
        a = axial_rope(a, *pos)
    return jnp.moveaxis(a, 1, 2).astype(jnp.float32)


def retention_scan(q, k, v, log_g, r0):
    B, H, T, _ = q.shape
    dv = v.shape[-1]
    n = T // RET_CHUNK
    chunk = lambda a: jnp.moveaxis(a.reshape(B, H, n, RET_CHUNK, a.shape[-1]), 2, 0)
    i = jnp.arange(RET_CHUNK, dtype=jnp.float32)
    diff = i[:, None] - i[None, :]
    dmask = jnp.where(diff >= 0, jnp.exp(log_g[:, None, None] * jnp.maximum(diff, 0.0)), 0.0)[None]
    xi = jnp.exp(log_g[:, None] * (i + 1.0))[None, :, :, None]
    zeta = jnp.exp(log_g[:, None] * (RET_CHUNK - 1.0 - i))[None, :, :, None]
    g_c = jnp.exp(log_g * RET_CHUNK)[None, :, None, None]

    def step(r, blk):
        qb, kb, vb = blk
        s = jnp.einsum('bhqd,bhkd->bhqk', qb, kb) * dmask
        y = jnp.einsum('bhqk,bhke->bhqe', s, vb) + jnp.einsum('bhqd,bhde->bhqe', qb * xi, r)
        r = g_c * r + jnp.einsum('bhkd,bhke->bhde', kb * zeta, vb)
        return r, y

    _, ys = lax.scan(step, r0, (chunk(q), chunk(k), chunk(v)))
    return jnp.moveaxis(ys, 0, 2).reshape(B, H, T, dv)


def context_state(k, v, log_g, reverse):
    L = k.shape[2]
    m = jnp.arange(L, dtype=jnp.float32)
    expo = m if reverse else (L - 1.0) - m
    w = jnp.exp(log_g[:, None] * expo[None, :])
    return jnp.einsum('bhld,bhle->bhde', k * w[None, :, :, None], v)


def retention_bidir(q, k, v, log_f, log_b, r_f, r_b):
    flip = lambda a: a[:, :, ::-1]
    y_f = retention_scan(q, k, v, log_f, r_f)
    y_b = flip(retention_scan(flip(q), flip(k), flip(v), log_b, r_b))
    return y_f + y_b


def ret_output(y, gate):
    B, H, T, dv = y.shape
    mu = y.mean(-1, keepdims=True)
    var = jnp.square(y - mu).mean(-1, keepdims=True)
    yn = jnp.moveaxis((y - mu) * lax.rsqrt(var + NORM_EPS), 1, 2).reshape(B, T, H * dv)
    return (jax.nn.silu(gate.astype(jnp.float32)) * yn).astype(gate.dtype)


def gqa_q(q, p, pos):
    B, T, _ = q.shape
    q = rms_norm(q.reshape(B, T, GQA_HEADS, GQA_DH), p['gqa_q_norm'])
    if pos is not None:
        q = axial_rope(q, *pos)
    return q.reshape(B, T, GQA_KV_HEADS, GQA_GROUP, GQA_DH)


def gqa_kv(k, v, p, pos):
    B, T, _ = k.shape
    k = rms_norm(k.reshape(B, T, GQA_KV_HEADS, GQA_DH), p['gqa_k_norm'])
    if pos is not None:
        k = axial_rope(k, *pos)
    return k, v.reshape(B, T, GQA_KV_HEADS, GQA_DH)


def merge_branches(o_a, o_b, o_c, g_a, g_b, g_c, p):
    m = (jax.nn.sigmoid(g_a) * (o_a @ p['w_br_a'])
         + jax.nn.sigmoid(g_b) * (o_b @ p['w_br_b'])
         + jax.nn.sigmoid(g_c) * (o_c @ p['w_br_c']))
    return m @ p['w_out']


def token_mixer(h, hc, p, pos, with_ctx_out):
    pl = split_in(h @ p['w_in'])
    pc = split_in(hc @ p['w_in'])
    cat = lambda a, b: jnp.concatenate([a, b], axis=1)
    ka_c, va_c = mla_kv(pc[1], pc[2], p, None)
    ka, va = mla_kv(pl[1], pl[2], p, pos)
    o_a = block_attention(mla_q(pl[0], p, pos), cat(ka_c, ka), cat(va_c, va), MLA_SCALE)
    log_f = jax.nn.log_sigmoid(p['ret_decay_logit'][0].astype(jnp.float32))
    log_b = jax.nn.log_sigmoid(p['ret_decay_logit'][1].astype(jnp.float32))
    kb_c = ret_heads(pc[4], RET_DK, None) * RET_K_SCALE
    vb_c = ret_heads(pc[5], RET_DV, None)
    r_f = context_state(kb_c, vb_c, log_f, False)
    r_b = context_state(kb_c, vb_c, log_b, True)
    y_b = retention_bidir(ret_heads(pl[3], RET_DK, pos), ret_heads(pl[4], RET_DK, pos) * RET_K_SCALE,
                          ret_heads(pl[5], RET_DV, None), log_f, log_b, r_f, r_b)
    o_b = ret_output(y_b, pl[6])
    kc_c, vc_c = gqa_kv(pc[8], pc[9], p, None)
    kc, vc = gqa_kv(pl[8], pl[9], p, pos)
    o_c = block_attention(gqa_q(pl[7], p, pos), cat(kc_c, kc), cat(vc_c, vc), GQA_SCALE)
    out = merge_branches(o_a, o_b, o_c, pl[10], pl[11], pl[12], p)
    if not with_ctx_out:
        return out, None
    oa_c = block_attention(mla_q(pc[0], p, None), ka_c, va_c, MLA_SCALE)
    zeros = jnp.zeros_like(r_f)
    ob_c = ret_output(retention_bidir(ret_heads(pc[3], RET_DK, None), kb_c, vb_c,
                                      log_f, log_b, zeros, zeros), pc[6])
    oc_c = block_attention(gqa_q(pc[7], p, None), kc_c, vc_c, GQA_SCALE)
    return out, merge_branches(oa_c, ob_c, oc_c, pc[10], pc[11], pc[12], p)


def peer_ffn(h, p):
    B, T, D = h.shape
    q = (h @ p['peer_w_q']).reshape(B, T, PEER_HEADS, PEER_DQ)
    half = PEER_DQ // 2
    s1 = jnp.einsum('bthd,kd->bthk', q[..., :half], p['peer_k1']).astype(jnp.float32)
    s2 = jnp.einsum('bthd,kd->bthk', q[..., half:], p['peer_k2']).astype(jnp.float32)
    v1, i1 = lax.top_k(s1, PEER_TOPK)
    v2, i2 = lax.top_k(s2, PEER_TOPK)
    cand_s = (v1[..., :, None] + v2[..., None, :]).reshape(B, T, PEER_HEADS, PEER_TOPK * PEER_TOPK)
    cand_e = (i1[..., :, None] * PEER_N_KEYS + i2[..., None, :]).reshape(B, T, PEER_HEADS, PEER_TOPK * PEER_TOPK)
    top_s, top_j = lax.top_k(cand_s, PEER_TOPK)
    experts = jnp.take_along_axis(cand_e, top_j, axis=-1)
    gates = jax.nn.softmax(top_s, axis=-1)
    nb = (B * T) // TOKEN_BLOCK
    kk = PEER_HEADS * PEER_TOPK
    hb = h.reshape(nb, TOKEN_BLOCK, D)
    eb = experts.reshape(nb, TOKEN_BLOCK, kk)
    gb = gates.reshape(nb, TOKEN_BLOCK, kk).astype(h.dtype)
    u_tab, v_tab = p['peer_u'], p['peer_v']

    def one_block(args):
        hx, ex, gx = args
        u = jnp.take(u_tab, ex, axis=0)
        a = jax.nn.gelu(jnp.einsum('nd,nkd->nk', hx, u), approximate=False) * gx
        return jnp.einsum('nk,nkd->nd', a, jnp.take(v_tab, ex, axis=0))

    return lax.map(one_block, (hb, eb, gb)).reshape(B, T, D)


def setup_inputs(seed: int = 0) -> dict:
    key = jax.random.key(seed)
    ks = jax.random.split(key, 32)
    L, D = DEPTH, D_MODEL
    nrm = lambda k, shape, s: jax.random.normal(k, shape, jnp.float32) * s
    gain = lambda k, n: 1.0 + nrm(k, (L, n), 0.01)
    decay_base = jnp.log(2.0 ** (5.0 + jnp.arange(RET_HEADS, dtype=jnp.float32)) - 1.0)
    return {
        'x': nrm(ks[0], (BATCH, SEQ, D), 1.0),
        'c': nrm(ks[1], (BATCH, D), 1.0),
        'ctx': nrm(ks[2], (BATCH, CTX_LEN, D), 1.0),
        'c_ctx': nrm(ks[3], (D,), 1.0),
        'w_mod': nrm(ks[4], (L, D, 6 * D), 0.5 * D ** -0.5),
        'b_mod': nrm(ks[5], (L, 6 * D), 0.01),
        'w_in': nrm(ks[6], (L, D, N_IN), D ** -0.5),
        'mla_q_norm': gain(ks[7], MLA_Q_LORA),
        'mla_w_qup': nrm(ks[8], (L, MLA_Q_LORA, MLA_HEADS * (MLA_NOPE + MLA_ROPE)), MLA_Q_LORA ** -0.5),
        'mla_kv_norm': gain(ks[9], MLA_KV_LORA),
        'mla_w_kvup': nrm(ks[10], (L, MLA_KV_LORA, MLA_HEADS * (MLA_NOPE + MLA_V)), MLA_KV_LORA ** -0.5),
        'ret_decay_logit': decay_base[None, None, :] + nrm(ks[11], (L, 2, RET_HEADS), 0.01),
        'gqa_q_norm': gain(ks[12], GQA_DH),
        'gqa_k_norm': gain(ks[13], GQA_DH),
        'w_br_a': nrm(ks[14], (L, MLA_HEADS * MLA_V, D), (MLA_HEADS * MLA_V) ** -0.5),
        'w_br_b': nrm(ks[15], (L, RET_HEADS * RET_DV, D), (RET_HEADS * RET_DV) ** -0.5),
        'w_br_c': nrm(ks[16], (L, GQA_HEADS * GQA_DH, D), (GQA_HEADS * GQA_DH) ** -0.5),
        'w_out': nrm(ks[17], (L, D, D), DEEPNORM_BETA * D ** -0.5),
        'ln1_g': gain(ks[18], D),
        'ln1_b': nrm(ks[19], (L, D), 0.01),
        'peer_w_q': nrm(ks[20], (L, D, PEER_HEADS * PEER_DQ), D ** -0.5),
        'peer_k1': nrm(ks[21], (L, PEER_N_KEYS, PEER_DQ // 2), (PEER_DQ // 2) ** -0.5),
        'peer_k2': nrm(ks[22], (L, PEER_N_KEYS, PEER_DQ // 2), (PEER_DQ // 2) ** -0.5),
        'peer_u': nrm(ks[23], (L, PEER_N_EXPERTS, D), D ** -0.5),
        'peer_v': nrm(ks[24], (L, PEER_N_EXPERTS, D), DEEPNORM_BETA),
        'ln2_g': gain(ks[25], D),
        'ln2_b': nrm(ks[26], (L, D), 0.01),
    }


def reference(x, c, ctx, c_ctx, w_mod, b_mod, w_in, mla_q_norm, mla_w_qup, mla_kv_norm, mla_w_kvup,
              ret_decay_logit, gqa_q_norm, gqa_k_norm, w_br_a, w_br_b, w_br_c, w_out, ln1_g, ln1_b,
              peer_w_q, peer_k1, peer_k2, peer_u, peer_v, ln2_g, ln2_b):
    S = x.shape[1]
    ROWS = S // GRID_W
    row = jnp.repeat(jnp.arange(ROWS, dtype=jnp.int32), GRID_W)
    col = jnp.tile(jnp.arange(GRID_W, dtype=jnp.int32), ROWS)
    pos = (row, col)
    silu_c = jax.nn.silu(c)
    silu_cc = jax.nn.silu(c_ctx)
    xc = ctx
    for l in range(DEPTH):
        last = l == DEPTH - 1
        p = {
            'w_in': w_in[l], 'mla_q_norm': mla_q_norm[l], 'mla_w_qup': mla_w_qup[l],
            'mla_kv_norm': mla_kv_norm[l], 'mla_w_kvup': mla_w_kvup[l],
            'ret_decay_logit': ret_decay_logit[l], 'gqa_q_norm': gqa_q_norm[l], 'gqa_k_norm': gqa_k_norm[l],
            'w_br_a': w_br_a[l], 'w_br_b': w_br_b[l], 'w_br_c': w_br_c[l], 'w_out': w_out[l],
            'peer_w_q': peer_w_q[l], 'peer_k1': peer_k1[l], 'peer_k2': peer_k2[l],
            'peer_u': peer_u[l], 'peer_v': peer_v[l],
        }
        mod = silu_c @ w_mod[l] + b_mod[l]
        mod_c = silu_cc @ w_mod[l] + b_mod[l]
        sh1, sc1, g1, sh2, sc2, g2 = jnp.split(mod[:, None, :], 6, axis=-1)
        sh1c, sc1c, g1c, sh2c, sc2c, g2c = jnp.split(mod_c, 6, axis=-1)
        h = x * (1.0 + sc1) + sh1
        hc = xc * (1.0 + sc1c) + sh1c
        o, oc = token_mixer(h, hc, p, pos, not last)
        x = layer_norm(DEEPNORM_ALPHA * x + g1 * o, ln1_g[l], ln1_b[l])
        x = layer_norm(DEEPNORM_ALPHA * x + g2 * peer_ffn(x * (1.0 + sc2) + sh2, p), ln2_g[l], ln2_b[l])
        if not last:
            xc = layer_norm(DEEPNORM_ALPHA * xc + g1c * oc, ln1_g[l], ln1_b[l])
            xc = layer_norm(DEEPNORM_ALPHA * xc + g2c * peer_ffn(xc * (1.0 + sc2c) + sh2c, p), ln2_g[l], ln2_b[l])
    return x
```

```python
import functools

import numpy as np
import jax
import jax.numpy as jnp
from jax import lax
from jax.experimental import pallas as pl
from jax.experimental.pallas import tpu as pltpu

GRID_W = 64
ROPE_BASE = 10000.0
NORM_EPS = 1e-5
RMS_EPS = 1e-6
MLA_HEADS, MLA_Q_LORA, MLA_KV_LORA, MLA_NOPE, MLA_ROPE, MLA_V = 8, 512, 256, 64, 32, 64
MLA_SCALE = (MLA_NOPE + MLA_ROPE) ** -0.5
RET_HEADS, RET_DK, RET_DV, RET_CHUNK = 8, 32, 64, 128
RET_K_SCALE = RET_DK ** -0.5
GQA_HEADS, GQA_KV_HEADS, GQA_DH = 8, 2, 128
GQA_GROUP = GQA_HEADS // GQA_KV_HEADS
GQA_SCALE = GQA_DH ** -0.5
PEER_HEADS, PEER_DQ, PEER_N_KEYS, PEER_TOPK = 8, 128, 128, 16
PEER_N_EXPERTS = PEER_N_KEYS * PEER_N_KEYS

LANES = 128
MXU = jnp.bfloat16
F32 = jnp.float32
NEG = -1e30
SQRT_HALF = float(np.sqrt(0.5))

_SEG = (("ga", None), ("gb", None), ("gc", None), ("gq", GQA_HEADS * GQA_DH),
        ("cq", MLA_Q_LORA), ("rv", RET_HEADS * RET_DV), ("rg", RET_HEADS * RET_DV),
        ("ckv", MLA_KV_LORA), ("rq", RET_HEADS * RET_DK), ("rk", RET_HEADS * RET_DK),
        ("gk", GQA_KV_HEADS * GQA_DH), ("gv", GQA_KV_HEADS * GQA_DH), ("kr", LANES))


def _layout(d_model):
    seg, off = {}, 0
    widths = [(name, d_model if w is None else w) for name, w in _SEG]
    for name, w in sorted(widths, key=lambda nw: -nw[1]):
        assert off % w == 0, (name, off, w)
        seg[name] = (off, w)
        off += w
    total = -(-off // 512) * 512
    return seg, total


def _params(sem, vmem_mb=48):
    return pltpu.CompilerParams(dimension_semantics=sem, vmem_limit_bytes=vmem_mb << 20)


def _tile(n, pref):
    t = min(n, pref)
    while n % t:
        t //= 2
    return t


def _mod_kernel(c_ref, w_ref, b_ref, o_ref):
    c = c_ref[...]
    a = (c * jax.nn.sigmoid(c)).astype(MXU)
    o_ref[...] = jnp.dot(a, w_ref[...].astype(MXU), preferred_element_type=F32) + b_ref[...]


def _modulation(cvec, w_mod, b_mod):
    L, D, N6 = w_mod.shape
    R = cvec.shape[0]
    tn = _tile(N6, 1024)
    return pl.pallas_call(
        _mod_kernel,
        grid=(L, N6 // tn),
        in_specs=[pl.BlockSpec((R, D), lambda l, j: (0, 0)),
                  pl.BlockSpec((None, D, tn), lambda l, j: (l, 0, j)),
                  pl.BlockSpec((None, 1, tn), lambda l, j: (l, 0, j))],
        out_specs=pl.BlockSpec((None, R, tn), lambda l, j: (l, 0, j)),
        out_shape=jax.ShapeDtypeStruct((L, R, N6), F32),
        compiler_params=_params(("parallel", "parallel")),
    )(cvec, w_mod, b_mod.reshape(L, 1, N6))


def _inproj_kernel(x_ref, sh_ref, sc_ref, w_ref, o_ref, h_scr):
    @pl.when(pl.program_id(1) == 0)
    def _():
        h_scr[...] = (x_ref[...] * (1.0 + sc_ref[...]) + sh_ref[...]).astype(h_scr.dtype)

    o_ref[...] = jnp.dot(h_scr[...], w_ref[...], preferred_element_type=F32)


def _inproj(xs, mod3, w, row_of_tile, tm):
    n, D = xs.shape
    NP = w.shape[1]
    tn = 512
    return pl.pallas_call(
        _inproj_kernel,
        grid=(n // tm, NP // tn),
        in_specs=[pl.BlockSpec((tm, D), lambda i, j: (i, 0)),
                  pl.BlockSpec((None, 1, D), lambda i, j: (row_of_tile(i, tm), 0, 0)),
                  pl.BlockSpec((None, 1, D), lambda i, j: (row_of_tile(i, tm), 0, 1)),
                  pl.BlockSpec((D, tn), lambda i, j: (0, j))],
        out_specs=pl.BlockSpec((tm, tn), lambda i, j: (i, j)),
        out_shape=jax.ShapeDtypeStruct((n, NP), F32),
        scratch_shapes=[pltpu.VMEM((tm, D), MXU)],
        compiler_params=_params(("parallel", "arbitrary")),
    )(xs, mod3, mod3, w)


def _swap_pairs(x, nf):
    lane = lax.broadcasted_iota(jnp.int32, x.shape, 1)
    up = pltpu.roll(x, LANES - nf, 1)
    down = pltpu.roll(x, nf, 1)
    return jnp.where((lane & nf) == 0, up, down)


def _rope(x, cos, sin, nf):
    return x * cos + _swap_pairs(x, nf) * sin


def _rms(x, g):
    return x * lax.rsqrt(jnp.mean(x * x, axis=-1, keepdims=True) + RMS_EPS) * g


def _prep_kernel(cq_ref, ckv_ref, kr_ref, rq_ref, rk_ref, gq_ref, gk_ref, gv_ref,
                 qn_ref, kvn_ref, wq_ref, wk_ref, wv_ref, gqn_ref, gkn_ref,
                 ca_ref, sa_ref, cb_ref, sb_ref, cc_ref, sc_ref,
                 qa_ref, ka_ref, va_ref, qb_ref, kb_ref, qc_ref, kc_ref, vc_ref):
    cqn = _rms(cq_ref[...], qn_ref[...]).astype(MXU)
    q = jnp.dot(cqn, wq_ref[...], preferred_element_type=F32)
    kvn = _rms(ckv_ref[...], kvn_ref[...]).astype(MXU)
    k = jnp.dot(kvn, wk_ref[...], preferred_element_type=F32)
    va_ref[...] = jnp.dot(kvn, wv_ref[...], preferred_element_type=F32).astype(va_ref.dtype)
    ca, sa = ca_ref[...], sa_ref[...]
    kr = _rope(kr_ref[...], ca, sa, MLA_ROPE // 4)
    for h in range(MLA_HEADS):
        sl = slice(h * LANES, (h + 1) * LANES)
        qa_ref[:, sl] = (_rope(q[:, sl], ca, sa, MLA_ROPE // 4) * MLA_SCALE).astype(qa_ref.dtype)
        ka_ref[:, sl] = (k[:, sl] + kr).astype(ka_ref.dtype)
    for half in range(RET_HEADS * RET_DK // LANES):
        sl = slice(half * LANES, (half + 1) * LANES)
        cb, sb = cb_ref[:, sl], sb_ref[:, sl]
        qb_ref[:, sl] = _rope(rq_ref[:, sl], cb, sb, RET_DK // 4)
        kb_ref[:, sl] = _rope(rk_ref[:, sl], cb, sb, RET_DK // 4) * RET_K_SCALE
    cc, sc = cc_ref[...], sc_ref[...]
    for h in range(GQA_HEADS):
        sl = slice(h * LANES, (h + 1) * LANES)
        qc_ref[:, sl] = (_rope(_rms(gq_ref[:, sl], gqn_ref[...]), cc, sc, GQA_DH // 4)
                         * GQA_SCALE).astype(qc_ref.dtype)
    for h in range(GQA_KV_HEADS):
        sl = slice(h * LANES, (h + 1) * LANES)
        kc_ref[:, sl] = _rope(_rms(gk_ref[:, sl], gkn_ref[...]), cc, sc, GQA_DH // 4).astype(kc_ref.dtype)
    vc_ref[...] = gv_ref[...].astype(vc_ref.dtype)


def _prep(P, seg, wts, tabs, tab_of_tile, tm):
    n = P.shape[0]

    def pcol(name):
        off, w = seg[name]
        return pl.BlockSpec((tm, w), lambda i, o=off // w: (i, o))

    def const(a):
        return pl.BlockSpec(a.shape, lambda i: (0,) * a.ndim)

    def tab(a):
        return pl.BlockSpec((tm, a.shape[1]), lambda i: (tab_of_tile(i), 0))

    def rows(w):
        return pl.BlockSpec((tm, w), lambda i: (i, 0))

    consts = [wts["qn"], wts["kvn"], wts["wq"], wts["wk"], wts["wv"], wts["gqn"], wts["gkn"]]
    hq, hk = GQA_HEADS * LANES, GQA_KV_HEADS * LANES
    ha = MLA_HEADS * LANES
    hb = RET_HEADS * RET_DK
    outs = [(ha, MXU), (ha, MXU), (ha, MXU), (hb, F32), (hb, F32), (hq, MXU), (hk, MXU), (hk, MXU)]
    return pl.pallas_call(
        _prep_kernel,
        grid=(n // tm,),
        in_specs=[pcol(s) for s in ("cq", "ckv", "kr", "rq", "rk", "gq", "gk", "gv")]
        + [const(a) for a in consts] + [tab(a) for a in tabs],
        out_specs=[rows(w) for w, _ in outs],
        out_shape=[jax.ShapeDtypeStruct((n, w), dt) for w, dt in outs],
        compiler_params=_params(("parallel",)),
    )(*([P] * 8), *consts, *tabs)


def _attn_kernel(*refs, group, tk, has_lat):
    if has_lat:
        q_ref, kc_ref, vc_ref, kl_ref, vl_ref, o_ref = refs
    else:
        q_ref, kc_ref, vc_ref, o_ref = refs
    tq = q_ref.shape[0]
    if group > 1:
        q = jnp.concatenate([q_ref[:, g * LANES:(g + 1) * LANES] for g in range(group)], axis=0)
    else:
        q = q_ref[...]
    R = q.shape[0]

    def chunk(k, v, m, l, acc):
        s = lax.dot_general(q, k, (((1,), (1,)), ((), ())), preferred_element_type=F32)
        m_new = jnp.maximum(m, jnp.max(s, axis=-1, keepdims=True))
        a = jnp.exp(m - m_new)
        p = jnp.exp(s - m_new)
        l = a * l + jnp.sum(p, axis=-1, keepdims=True)
        acc = a * acc + jnp.dot(p.astype(MXU), v, preferred_element_type=F32)
        return m_new, l, acc

    carry = (jnp.full((R, 1), NEG, F32), jnp.zeros((R, 1), F32), jnp.zeros((R, LANES), F32))
    carry = chunk(kc_ref[...], vc_ref[...], *carry)
    if has_lat:
        def body(c, carry):
            off = pl.multiple_of(c * tk, tk)
            return chunk(kl_ref[pl.ds(off, tk), :], vl_ref[pl.ds(off, tk), :], *carry)

        carry = lax.fori_loop(0, kl_ref.shape[0] // tk, body, carry)
    _, l, acc = carry
    o = acc * (1.0 / l)
    for g in range(group):
        o_ref[:, g * LANES:(g + 1) * LANES] = o[g * tq:(g + 1) * tq].astype(o_ref.dtype)


def _attention(q, kc, vc, kl, vl, batch, group, rows_q):
    n, hq = q.shape
    hk = hq // (group * LANES)
    ctx_len = kc.shape[0] // batch
    has_lat = kl is not None
    tq = _tile(rows_q, 512 // group)
    nq = rows_q // tq
    in_specs = [pl.BlockSpec((tq, group * LANES), lambda b, h, i: (b * nq + i, h)),
                pl.BlockSpec((ctx_len, LANES), lambda b, h, i: (b, h)),
                pl.BlockSpec((ctx_len, LANES), lambda b, h, i: (b, h))]
    args = [q, kc, vc]
    tk = 512
    if has_lat:
        seq = kl.shape[0] // batch
        tk = _tile(seq, 512)
        in_specs += [pl.BlockSpec((seq, LANES), lambda b, h, i: (b, h))] * 2
        args += [kl, vl]
    return pl.pallas_call(
        functools.partial(_attn_kernel, group=group, tk=tk, has_lat=has_lat),
        grid=(batch, hk, nq),
        in_specs=in_specs,
        out_specs=pl.BlockSpec((tq, group * LANES), lambda b, h, i: (b * nq + i, h)),
        out_shape=jax.ShapeDtypeStruct((n, hq), MXU),
        compiler_params=_params(("parallel", "parallel", "arbitrary")),
    )(*args)


def _ret_kernel(q_ref, k_ref, v_ref, g_ref, dm_ref, xi_ref, zeta_ref, gc_ref, r0_ref,
                o_ref, rT_ref, r_scr, yf_scr):
    d = pl.program_id(1)
    j = pl.program_id(2)
    nch = pl.num_programs(2)
    C = RET_CHUNK

    @pl.when(j == 0)
    def _():
        r_scr[...] = r0_ref[...]

    q = q_ref[...]
    k = k_ref[...]
    vb = v_ref[...].astype(MXU)
    qb = q.astype(MXU)
    qx = (q * xi_ref[...]).astype(MXU)
    kT = k.T
    kTb = kT.astype(MXU)
    kzT = (kT * zeta_ref[...]).astype(MXU)
    ys = []
    for h in range(RET_HEADS):
        ks = slice(h * RET_DK, (h + 1) * RET_DK)
        vs = slice(h * RET_DV, (h + 1) * RET_DV)
        r_h = r_scr[ks, :]
        s = jnp.dot(qb[:, ks], kTb[ks, :], preferred_element_type=F32) * dm_ref[h]
        y = (jnp.dot(s.astype(MXU), vb[:, vs], preferred_element_type=F32)
             + jnp.dot(qx[:, ks], r_h.astype(MXU), preferred_element_type=F32))
        r_scr[ks, :] = gc_ref[ks, :] * r_h + jnp.dot(kzT[ks, :], vb[:, vs], preferred_element_type=F32)
        ys.append(y)

    chunk = jnp.where(d == 0, j, nch - 1 - j)
    row0 = pl.multiple_of(chunk * C, C)

    @pl.when(d == 0)
    def _():
        for h in range(RET_HEADS):
            yf_scr[pl.ds(row0, C), h * RET_DV:(h + 1) * RET_DV] = ys[h]

    @pl.when(d == 1)
    def _():
        for h in range(RET_HEADS):
            vs = slice(h * RET_DV, (h + 1) * RET_DV)
            y = ys[h] + yf_scr[pl.ds(row0, C), vs]
            mu = jnp.mean(y, axis=-1, keepdims=True)
            yc = y - mu
            yn = yc * lax.rsqrt(jnp.mean(yc * yc, axis=-1, keepdims=True) + NORM_EPS)
            g = g_ref[:, vs]
            o_ref[:, vs] = (g * jax.nn.sigmoid(g) * yn).astype(o_ref.dtype)

    @pl.when(j == nch - 1)
    def _():
        rT_ref[...] = r_scr[...]


def _retention(qb, kb, P, seg, dec, r0, batch):
    n = qb.shape[0]
    C = RET_CHUNK
    nch = n // batch // C
    hk, hv = RET_HEADS * RET_DK, RET_HEADS * RET_DV
    cv, cg = seg["rv"][0] // hv, seg["rg"][0] // hv

    def chunk_of(d, j):
        return jnp.where(d == 0, j, nch - 1 - j)

    def rows(b, d, j):
        return b * nch + chunk_of(d, j)

    def out_rows(b, d, j):
        return b * nch + jnp.where(d == 0, nch - 1, nch - 1 - j)

    return pl.pallas_call(
        _ret_kernel,
        grid=(batch, 2, nch),
        in_specs=[pl.BlockSpec((C, hk), lambda b, d, j: (rows(b, d, j), 0)),
                  pl.BlockSpec((C, hk), lambda b, d, j: (rows(b, d, j), 0)),
                  pl.BlockSpec((C, hv), lambda b, d, j: (rows(b, d, j), cv)),
                  pl.BlockSpec((C, hv), lambda b, d, j: (out_rows(b, d, j), cg)),
                  pl.BlockSpec((None, RET_HEADS, C, C), lambda b, d, j: (d, 0, 0, 0)),
                  pl.BlockSpec((None, C, hk), lambda b, d, j: (d, 0, 0)),
                  pl.BlockSpec((None, hk, C), lambda b, d, j: (d, 0, 0)),
                  pl.BlockSpec((None, hk, RET_DV), lambda b, d, j: (d, 0, 0)),
                  pl.BlockSpec((None, None, hk, RET_DV), lambda b, d, j: (b, d, 0, 0))],
        out_specs=[pl.BlockSpec((C, hv), lambda b, d, j: (out_rows(b, d, j), 0)),
                   pl.BlockSpec((None, None, hk, RET_DV), lambda b, d, j: (b, d, 0, 0))],
        out_shape=[jax.ShapeDtypeStruct((n, hv), MXU),
                   jax.ShapeDtypeStruct((batch, 2, hk, RET_DV), F32)],
        scratch_shapes=[pltpu.VMEM((hk, RET_DV), F32), pltpu.VMEM((nch * C, hv), F32)],
        compiler_params=_params(("parallel", "arbitrary", "arbitrary")),
    )(qb, kb, P, P, dec["dm"], dec["xi"], dec["zetaT"], dec["gc"], r0)


def _decay_tables(logit):
    C = RET_CHUNK
    log_g = jax.nn.log_sigmoid(logit.astype(F32))
    i = jnp.arange(C, dtype=F32)
    diff = i[:, None] - i[None, :]
    lg = log_g[:, :, None, None]
    fwd = jnp.where(diff >= 0, jnp.exp(lg[0] * jnp.maximum(diff, 0.0)), 0.0)
    bwd = jnp.where(diff <= 0, jnp.exp(lg[1] * jnp.maximum(-diff, 0.0)), 0.0)
    rep = lambda a: jnp.repeat(a, RET_DK, axis=0)
    xi_f = jnp.exp(log_g[0][:, None] * (i + 1.0))
    xi_b = jnp.exp(log_g[1][:, None] * (C - i))
    ze_f = jnp.exp(log_g[0][:, None] * (C - 1.0 - i))
    ze_b = jnp.exp(log_g[1][:, None] * i)
    gc = jnp.exp(log_g * C)
    return {
        "dm": jnp.stack([fwd, bwd]),
        "xi": jnp.stack([rep(xi_f).T, rep(xi_b).T]),
        "zetaT": jnp.stack([rep(ze_f), rep(ze_b)]),
        "gc": jnp.broadcast_to(jnp.repeat(gc, RET_DK, axis=1)[:, :, None],
                               (2, RET_HEADS * RET_DK, RET_DV)),
    }


def _merge_kernel(oa_ref, ob_ref, oc_ref, ga_ref, gb_ref, gc_ref, wa_ref, wb_ref, wc_ref, m_ref):
    m = (jax.nn.sigmoid(ga_ref[...]) * jnp.dot(oa_ref[...], wa_ref[...], preferred_element_type=F32)
         + jax.nn.sigmoid(gb_ref[...]) * jnp.dot(ob_ref[...], wb_ref[...], preferred_element_type=F32)
         + jax.nn.sigmoid(gc_ref[...]) * jnp.dot(oc_ref[...], wc_ref[...], preferred_element_type=F32))
    m_ref[...] = m.astype(m_ref.dtype)


def _merge(oa, ob, oc, P, seg, wa, wb, wc, tm):
    n = oa.shape[0]
    D = wa.shape[1]
    tn = _tile(D, 512)
    nb = D // tn

    def rows(a):
        return pl.BlockSpec((tm, a.shape[1]), lambda i, j: (i, 0))

    def gate(name):
        return pl.BlockSpec((tm, tn), lambda i, j, o=seg[name][0] // tn: (i, o + j))

    def wcol(a):
        return pl.BlockSpec((a.shape[0], tn), lambda i, j: (0, j))

    return pl.pallas_call(
        _merge_kernel,
        grid=(n // tm, nb),
        in_specs=[rows(oa), rows(ob), rows(oc), gate("ga"), gate("gb"), gate("gc"),
                  wcol(wa), wcol(wb), wcol(wc)],
        out_specs=pl.BlockSpec((tm, tn), lambda i, j: (i, j)),
        out_shape=jax.ShapeDtypeStruct((n, D), MXU),
        compiler_params=_params(("parallel", "arbitrary")),
    )(oa, ob, oc, P, P, P, wa, wb, wc)


def _layer_norm(y, g, b):
    mu = jnp.mean(y, axis=-1, keepdims=True)
    yc = y - mu
    return yc * lax.rsqrt(jnp.mean(yc * yc, axis=-1, keepdims=True) + NORM_EPS) * g + b


def _outln_kernel(m_ref, w_ref, x_ref, g1_ref, lg_ref, lb_ref, o_ref, *, alpha):
    o = jnp.dot(m_ref[...], w_ref[...], preferred_element_type=F32)
    o_ref[...] = _layer_norm(alpha * x_ref[...] + g1_ref[...] * o, lg_ref[...], lb_ref[...])


def _outln(m, w, xs, mod3, lg, lb, row_of_tile, alpha, tm):
    n, D = xs.shape
    vec = pl.BlockSpec((1, D), lambda i: (0, 0))
    return pl.pallas_call(
        functools.partial(_outln_kernel, alpha=alpha),
        grid=(n // tm,),
        in_specs=[pl.BlockSpec((tm, D), lambda i: (i, 0)),
                  pl.BlockSpec((D, D), lambda i: (0, 0)),
                  pl.BlockSpec((tm, D), lambda i: (i, 0)),
                  pl.BlockSpec((None, 1, D), lambda i: (row_of_tile(i, tm), 0, 2)),
                  vec, vec],
        out_specs=pl.BlockSpec((tm, D), lambda i: (i, 0)),
        out_shape=jax.ShapeDtypeStruct((n, D), F32),
        compiler_params=_params(("parallel",)),
    )(m, w, xs, mod3, lg, lb)


def _top16(s):
    vals = []
    work = s
    for a in range(PEER_TOPK):
        m = jnp.max(work, axis=0)
        vals.append(m)
        if a + 1 < PEER_TOPK:
            work = jnp.where(work == m[None], -jnp.inf, work)
    return vals


def _router_kernel(x_ref, sh_ref, sc_ref, wq_ref, k1_ref, k2_ref, k2h_ref,
                   ht_ref, s1_ref, e1_ref, s2_ref, e2_ref, t_ref, s1_scr, s2_scr):
    K, H = PEER_N_KEYS, PEER_HEADS
    tm = x_ref.shape[0]
    h = x_ref[...] * (1.0 + sc_ref[...]) + sh_ref[...]
    hT = h.T.astype(MXU)
    ht_ref[...] = hT
    qT = jnp.dot(wq_ref[...], hT, preferred_element_type=F32).astype(MXU)
    s1_scr[...] = jnp.dot(k1_ref[...], qT, preferred_element_type=F32).reshape(K, H, tm)
    s2_scr[...] = jnp.dot(k2_ref[...], qT, preferred_element_type=F32).reshape(K, H, tm)
    s2_ref[...] = jnp.dot(k2h_ref[...], qT, preferred_element_type=F32).reshape(H, K, tm)
    s1_ref[...] = s1_scr[...]

    def lane_block(c, _):
        ls = pl.ds(pl.multiple_of(c * LANES, LANES), LANES)
        s1 = s1_scr[:, :, ls]
        v1 = _top16(s1)
        v2 = _top16(s2_scr[:, :, ls])
        pairs = [(a, b) for a in range(PEER_TOPK) for b in range(PEER_TOPK)
                 if (a + 1) * (b + 1) <= PEER_TOPK]
        cand = [v1[a] + v2[b] for a, b in pairs]
        work = list(cand)
        t = None
        for r in range(PEER_TOPK):
            t = functools.reduce(jnp.maximum, work)
            if r + 1 < PEER_TOPK:
                work = [jnp.where(w == t, -jnp.inf, w) for w in work]
        ex1 = [jnp.exp(v - v1[0]) for v in v1]
        ex2 = [jnp.exp(v - v2[0]) for v in v2]
        z = jnp.zeros_like(t)
        for (a, b), cv in zip(pairs, cand):
            z = z + jnp.where(cv >= t, ex1[a] * ex2[b], 0.0)
        inv_z = 1.0 / z
        t_ref[:, ls] = t
        e1_ref[:, :, ls] = jnp.where(s1 >= v1[-1][None], jnp.exp(s1 - v1[0][None]), 0.0)
        for hh in range(H):
            s2h = s2_ref[hh, :, ls]
            thr = v2[-1][hh:hh + 1]
            e2_ref[hh, :, ls] = jnp.where(
                s2h >= thr, jnp.exp(s2h - v2[0][hh:hh + 1]) * inv_z[hh:hh + 1], 0.0)
        return 0

    lax.fori_loop(0, tm // LANES, lane_block, 0)


def _router(x1, mod3, wts, row_of_tile, tm):
    n, D = x1.shape
    K, H = PEER_N_KEYS, PEER_HEADS
    HQ = H * PEER_DQ

    def const(a):
        return pl.BlockSpec(a.shape, lambda i: (0,) * a.ndim)

    kh = pl.BlockSpec((K, H, tm), lambda i: (0, 0, i))
    hk = pl.BlockSpec((H, K, tm), lambda i: (0, 0, i))
    return pl.pallas_call(
        _router_kernel,
        grid=(n // tm,),
        in_specs=[pl.BlockSpec((tm, D), lambda i: (i, 0)),
                  pl.BlockSpec((None, 1, D), lambda i: (row_of_tile(i, tm), 0, 3)),
                  pl.BlockSpec((None, 1, D), lambda i: (row_of_tile(i, tm), 0, 4)),
                  const(wts["wqT"]), const(wts["k1kh"]), const(wts["k2kh"]), const(wts["k2hk"])],
        out_specs=[pl.BlockSpec((D, tm), lambda i: (0, i)), kh, kh, hk, hk,
                   pl.BlockSpec((H, tm), lambda i: (0, i))],
        out_shape=[jax.ShapeDtypeStruct((D, n), MXU),
                   jax.ShapeDtypeStruct((K, H, n), F32), jax.ShapeDtypeStruct((K, H, n), F32),
                   jax.ShapeDtypeStruct((H, K, n), F32), jax.ShapeDtypeStruct((H, K, n), F32),
                   jax.ShapeDtypeStruct((H, n), F32)],
        scratch_shapes=[pltpu.VMEM((K, H, tm), F32), pltpu.VMEM((K, H, tm), F32)],
        compiler_params=_params(("parallel",)),
    )(x1, mod3, mod3, wts["wqT"], wts["k1kh"], wts["k2kh"], wts["k2hk"])


def _peer_kernel(u_ref, vt_ref, ht_ref, s1_ref, e1_ref, s2_ref, e2_ref, t_ref,
                 x_ref, g2_ref, lg_ref, lb_ref, o_ref, acc, *, alpha):
    e = pl.program_id(1)
    te = u_ref.shape[0]
    K = PEER_N_KEYS
    sc = jnp.dot(u_ref[...], ht_ref[...], preferred_element_type=F32)
    act = 0.5 * sc * (1.0 + lax.erf(sc * SQRT_HALF))
    t = t_ref[...]
    parts = []
    for sub in range(te // K):
        i1 = e * (te // K) + sub
        s1r = s1_ref[i1]
        e1r = e1_ref[i1]
        g = None
        for h in range(PEER_HEADS):
            hit = (s1r[h:h + 1] + s2_ref[h]) >= t[h:h + 1]
            w = jnp.where(hit, e2_ref[h] * e1r[h:h + 1], 0.0)
            g = w if g is None else g + w
        parts.append((act[sub * K:(sub + 1) * K] * g).astype(MXU))
    a = jnp.concatenate(parts, axis=0) if len(parts) > 1 else parts[0]
    contrib = jnp.dot(vt_ref[...], a, preferred_element_type=F32)

    @pl.when(e == 0)
    def _():
        acc[...] = contrib

    @pl.when(e > 0)
    def _():
        acc[...] += contrib

    @pl.when(e == pl.num_programs(1) - 1)
    def _():
        y = alpha * x_ref[...] + g2_ref[...] * acc[...].T
        o_ref[...] = _layer_norm(y, lg_ref[...], lb_ref[...])


def _peer(x1, routed, u, vt, mod3, lg, lb, row_of_tile, alpha, tm):
    n, D = x1.shape
    K, H = PEER_N_KEYS, PEER_HEADS
    E = u.shape[0]
    te = 512
    ht, s1, e1, s2, e2, t = routed
    once = dict(pipeline_mode=pl.Buffered(1))
    kh = pl.BlockSpec((K, H, tm), lambda i, e: (0, 0, i), **once)
    hk = pl.BlockSpec((H, K, tm), lambda i, e: (0, 0, i), **once)
    vec = pl.BlockSpec((1, D), lambda i, e: (0, 0))
    return pl.pallas_call(
        functools.partial(_peer_kernel, alpha=alpha),
        grid=(n // tm, E // te),
        in_specs=[pl.BlockSpec((te, D), lambda i, e: (e, 0)),
                  pl.BlockSpec((D, te), lambda i, e: (0, e)),
                  pl.BlockSpec((D, tm), lambda i, e: (0, i), **once),
                  kh, kh, hk, hk,
                  pl.BlockSpec((H, tm), lambda i, e: (0, i)),
                  pl.BlockSpec((tm, D), lambda i, e: (i, 0), **once),
                  pl.BlockSpec((None, 1, D), lambda i, e: (row_of_tile(i, tm), 0, 5)),
                  vec, vec],
        out_specs=pl.BlockSpec((tm, D), lambda i, e: (i, 0)),
        out_shape=jax.ShapeDtypeStruct((n, D), F32),
        scratch_shapes=[pltpu.VMEM((D, tm), F32)],
        compiler_params=_params(("parallel", "arbitrary"), 56),
    )(u, vt, ht, s1, e1, s2, e2, t, x1, mod3, lg, lb)


def _pad_heads(w, heads, width, lo=0):
    lead = w.shape[:-1]
    w = w.reshape(*lead, heads, width)
    w = jnp.pad(w, [(0, 0)] * len(lead) + [(0, 0), (lo, LANES - lo - width)])
    return w.reshape(*lead, heads * LANES)


def _layer_weights(l, seg, total, w_in, mla_q_norm, mla_w_qup, mla_kv_norm, mla_w_kvup, gqa_q_norm,
                   gqa_k_norm, w_br_a, w_br_b, w_br_c, w_out, peer_w_q, peer_k1, peer_k2, peer_u, peer_v):
    D = w_in.shape[1]
    widths = (MLA_Q_LORA, MLA_KV_LORA, MLA_ROPE, RET_HEADS * RET_DK, RET_HEADS * RET_DK,
              RET_HEADS * RET_DV, RET_HEADS * RET_DV, GQA_HEADS * GQA_DH, GQA_KV_HEADS * GQA_DH,
              GQA_KV_HEADS * GQA_DH, D, D, D)
    names = ("cq", "ckv", "kr", "rq", "rk", "rv", "rg", "gq", "gk", "gv", "ga", "gb", "gc")
    cols = jnp.split(w_in[l], np.cumsum(widths)[:-1].tolist(), axis=1)
    win = jnp.zeros((D, total), MXU)
    for name, col in zip(names, cols):
        off = seg[name][0] + (MLA_NOPE if name == "kr" else 0)
        win = lax.dynamic_update_slice(win, col.astype(MXU), (0, off))
    kv = mla_w_kvup[l].reshape(MLA_KV_LORA, MLA_HEADS, MLA_NOPE + MLA_V)
    eye = jnp.eye(PEER_HEADS, dtype=F32)
    half = PEER_DQ // 2
    k1p = jnp.pad(peer_k1[l], ((0, 0), (0, half)))
    k2p = jnp.pad(peer_k2[l], ((0, 0), (half, 0)))
    HQ = PEER_HEADS * PEER_DQ
    return {
        "win": win,
        "qn": mla_q_norm[l][None], "kvn": mla_kv_norm[l][None],
        "wq": _pad_heads(mla_w_qup[l], MLA_HEADS, MLA_NOPE + MLA_ROPE).astype(MXU),
        "wk": _pad_heads(kv[..., :MLA_NOPE].reshape(MLA_KV_LORA, -1), MLA_HEADS, MLA_NOPE).astype(MXU),
        "wv": _pad_heads(kv[..., MLA_NOPE:].reshape(MLA_KV_LORA, -1), MLA_HEADS, MLA_V).astype(MXU),
        "gqn": gqa_q_norm[l][None], "gkn": gqa_k_norm[l][None],
        "wa": _pad_heads(w_br_a[l].T, MLA_HEADS, MLA_V).T.astype(MXU),
        "wb": w_br_b[l].astype(MXU), "wc": w_br_c[l].astype(MXU), "wo": w_out[l].astype(MXU),
        "wqT": peer_w_q[l].T.astype(MXU),
        "k1kh": jnp.einsum("kd,hg->khgd", k1p, eye).reshape(HQ, HQ).astype(MXU),
        "k2kh": jnp.einsum("kd,hg->khgd", k2p, eye).reshape(HQ, HQ).astype(MXU),
        "k2hk": jnp.einsum("kd,hg->hkgd", k2p, eye).reshape(HQ, HQ).astype(MXU),
        "u": peer_u[l].astype(MXU), "vt": peer_v[l].T.astype(MXU),
    }


def _rope_tables(seq):
    t = jnp.arange(seq, dtype=jnp.int32)
    row, col = (t // GRID_W).astype(F32), (t % GRID_W).astype(F32)

    def tab(r):
        nf = r // 4
        inv = ROPE_BASE ** (-jnp.arange(nf, dtype=F32) / nf)
        ar, ac = row[:, None] * inv[None], col[:, None] * inv[None]
        cos = jnp.concatenate([jnp.cos(ar)] * 2 + [jnp.cos(ac)] * 2, axis=1)
        sin = jnp.concatenate([-jnp.sin(ar), jnp.sin(ar), -jnp.sin(ac), jnp.sin(ac)], axis=1)
        return cos, sin

    ca, sa = tab(MLA_ROPE)
    lo, hi = MLA_NOPE, LANES - MLA_NOPE - MLA_ROPE
    ca = jnp.pad(ca, ((0, 0), (lo, hi)), constant_values=1.0)
    sa = jnp.pad(sa, ((0, 0), (lo, hi)))
    cb, sb = tab(RET_DK)
    cb, sb = jnp.tile(cb, (1, RET_HEADS)), jnp.tile(sb, (1, RET_HEADS))
    cc, sc = tab(GQA_DH)
    return [ca, sa, cb, sb, cc, sc]


def kernel(x, c, ctx, c_ctx, w_mod, b_mod, w_in, mla_q_norm, mla_w_qup, mla_kv_norm, mla_w_kvup, ret_decay_logit, gqa_q_norm, gqa_k_norm, w_br_a, w_br_b, w_br_c, w_out, ln1_g, ln1_b, peer_w_q, peer_k1, peer_k2, peer_u, peer_v, ln2_g, ln2_b):
    B, S, D = x.shape
    CTX = ctx.shape[1]
    L = w_in.shape[0]
    alpha = float((2.0 * L) ** 0.25)
    seg, total = _layout(D)

    tm_l = _tile(S, 512)
    tm_c = _tile(B * CTX, 512)
    tpb = S // tm_l
    lat_row = lambda i, tm: (i * tm) // S
    ctx_row = lambda i, tm: B
    lat_tab = lambda i: i % tpb
    ctx_tab = lambda i: 0

    nrow = -(-(B + 1) // 8) * 8
    cvec = jnp.concatenate([c, c_ctx[None], jnp.zeros((nrow - B - 1, D), F32)], axis=0)
    mod = _modulation(cvec, w_mod, b_mod)

    tabs_l = _rope_tables(S)
    tabs_c = [jnp.ones((tm_c, a.shape[1]), F32) if k % 2 == 0 else jnp.zeros((tm_c, a.shape[1]), F32)
              for k, a in enumerate(tabs_l)]

    xl = x.reshape(B * S, D)
    xc = ctx.reshape(B * CTX, D)
    for l in range(L):
        last = l == L - 1
        W = _layer_weights(l, seg, total, w_in, mla_q_norm, mla_w_qup, mla_kv_norm, mla_w_kvup,
                           gqa_q_norm, gqa_k_norm, w_br_a, w_br_b, w_br_c, w_out,
                           peer_w_q, peer_k1, peer_k2, peer_u, peer_v)
        dec = _decay_tables(ret_decay_logit[l])
        mod3 = mod[l].reshape(nrow, 1, 6 * D)
        lg1, lb1, lg2, lb2 = ln1_g[l][None], ln1_b[l][None], ln2_g[l][None], ln2_b[l][None]

        Pl = _inproj(xl, mod3, W["win"], lat_row, tm_l)
        Pc = _inproj(xc, mod3, W["win"], ctx_row, tm_c)
        qa, ka, va, qb, kb, qc, kc, vc = _prep(Pl, seg, W, tabs_l, lat_tab, tm_l)
        qa_c, ka_c, va_c, qb_c, kb_c, qc_c, kc_c, vc_c = _prep(Pc, seg, W, tabs_c, ctx_tab, tm_c)

        oa = _attention(qa, ka_c, va_c, ka, va, B, 1, S)
        oc = _attention(qc, kc_c, vc_c, kc, vc, B, GQA_GROUP, S)
        r0 = jnp.zeros((B, 2, RET_HEADS * RET_DK, RET_DV), F32)
        ob_c, r_ctx = _retention(qb_c, kb_c, Pc, seg, dec, r0, B)
        ob, _ = _retention(qb, kb, Pl, seg, dec, r_ctx, B)

        def tail(xs, oa, ob, oc, P, row_of_tile, tm):
            m = _merge(oa, ob, oc, P, seg, W["wa"], W["wb"], W["wc"], tm)
            x1 = _outln(m, W["wo"], xs, mod3, lg1, lb1, row_of_tile, alpha, _tile(tm, 256))
            routed = _router(x1, mod3, W, row_of_tile, tm)
            return _peer(x1, routed, W["u"], W["vt"], mod3, lg2, lb2, row_of_tile, alpha, tm)

        if not last:
            oa_c = _attention(qa_c, ka_c, va_c, None, None, B, 1, CTX)
            oc_c = _attention(qc_c, kc_c, vc_c, None, None, B, GQA_GROUP, CTX)
            xc = tail(xc, oa_c, ob_c, oc_c, Pc, ctx_row, tm_c)
        xl = tail(xl, oa, ob, oc, Pl, lat_row, tm_l)
    return xl.reshape(B, S, D)
```

```python
import functools

import numpy as np
import jax
import jax.numpy as jnp
from jax import lax
from jax.experimental import pallas as pl
from jax.experimental.pallas import tpu as pltpu

GRID_W = 64
ROPE_BASE = 10000.0
NORM_EPS = 1e-5
RMS_EPS = 1e-6
MLA_HEADS, MLA_Q_LORA, MLA_KV_LORA, MLA_NOPE, MLA_ROPE, MLA_V = 8, 512, 256, 64, 32, 64
MLA_SCALE = (MLA_NOPE + MLA_ROPE) ** -0.5
RET_HEADS, RET_DK, RET_DV, RET_CHUNK = 8, 32, 64, 128
RET_K_SCALE = RET_DK ** -0.5
GQA_HEADS, GQA_KV_HEADS, GQA_DH = 8, 2, 128
GQA_GROUP = GQA_HEADS // GQA_KV_HEADS
GQA_SCALE = GQA_DH ** -0.5
PEER_HEADS, PEER_DQ, PEER_N_KEYS, PEER_TOPK = 8, 128, 128, 16
PEER_N_EXPERTS = PEER_N_KEYS * PEER_N_KEYS

LANES = 128
MXU = jnp.bfloat16
F32 = jnp.float32
NEG = -1e30
SQRT_HALF = float(np.sqrt(0.5))
LOG2E = float(np.log2(np.e))

_SEG = (("ga", None), ("gb", None), ("gc", None), ("gq", GQA_HEADS * GQA_DH),
        ("cq", MLA_Q_LORA), ("rv", RET_HEADS * RET_DV), ("rg", RET_HEADS * RET_DV),
        ("ckv", MLA_KV_LORA), ("rq", RET_HEADS * RET_DK), ("rk", RET_HEADS * RET_DK),
        ("gk", GQA_KV_HEADS * GQA_DH), ("gv", GQA_KV_HEADS * GQA_DH), ("kr", LANES))


def _layout(d_model):
    seg, off = {}, 0
    widths = [(name, d_model if w is None else w) for name, w in _SEG]
    for name, w in sorted(widths, key=lambda nw: -nw[1]):
        assert off % w == 0, (name, off, w)
        seg[name] = (off, w)
        off += w
    total = -(-off // 512) * 512
    return seg, total


def _params(sem, vmem_mb=48, flags=None):
    return pltpu.CompilerParams(dimension_semantics=sem, vmem_limit_bytes=vmem_mb << 20, flags=flags)


def _tile(n, pref):
    t = min(n, pref)
    while n % t:
        t //= 2
    return t


def _mod_kernel(c_ref, w_ref, b_ref, o_ref):
    c = c_ref[...]
    a = (c * jax.nn.sigmoid(c)).astype(MXU)
    o_ref[...] = jnp.dot(a, w_ref[...].astype(MXU), preferred_element_type=F32) + b_ref[...]


def _modulation(cvec, w_mod, b_mod):
    L, D, N6 = w_mod.shape
    R = cvec.shape[0]
    tn = _tile(N6, 1024)
    return pl.pallas_call(
        _mod_kernel,
        name="mod",
        grid=(L, N6 // tn),
        in_specs=[pl.BlockSpec((R, D), lambda l, j: (0, 0)),
                  pl.BlockSpec((None, D, tn), lambda l, j: (l, 0, j)),
                  pl.BlockSpec((None, 1, tn), lambda l, j: (l, 0, j))],
        out_specs=pl.BlockSpec((None, R, tn), lambda l, j: (l, 0, j)),
        out_shape=jax.ShapeDtypeStruct((L, R, N6), F32),
        compiler_params=_params(("parallel", "parallel")),
    )(cvec, w_mod, b_mod.reshape(L, 1, N6))


def _inproj_kernel(x_ref, sh_ref, sc_ref, w_ref, o_ref, h_scr):
    @pl.when(pl.program_id(1) == 0)
    def _():
        h_scr[...] = (x_ref[...] * (1.0 + sc_ref[...]) + sh_ref[...]).astype(h_scr.dtype)

    o_ref[...] = jnp.dot(h_scr[...], w_ref[...], preferred_element_type=F32)


def _inproj(xs, mod3, w, row_of_tile, tm):
    n, D = xs.shape
    NP = w.shape[1]
    tn = 512
    return pl.pallas_call(
        _inproj_kernel,
        name="inproj",
        grid=(n // tm, NP // tn),
        in_specs=[pl.BlockSpec((tm, D), lambda i, j: (i, 0)),
                  pl.BlockSpec((None, 1, D), lambda i, j: (row_of_tile(i, tm), 0, 0)),
                  pl.BlockSpec((None, 1, D), lambda i, j: (row_of_tile(i, tm), 0, 1)),
                  pl.BlockSpec((D, tn), lambda i, j: (0, j))],
        out_specs=pl.BlockSpec((tm, tn), lambda i, j: (i, j)),
        out_shape=jax.ShapeDtypeStruct((n, NP), F32),
        scratch_shapes=[pltpu.VMEM((tm, D), MXU)],
        compiler_params=_params(("parallel", "arbitrary")),
    )(xs, mod3, mod3, w)


def _swap_pairs(x, nf):
    lane = lax.broadcasted_iota(jnp.int32, x.shape, 1)
    up = pltpu.roll(x, LANES - nf, 1)
    down = pltpu.roll(x, nf, 1)
    return jnp.where((lane & nf) == 0, up, down)


def _rope(x, cos, sin, nf):
    return x * cos + _swap_pairs(x, nf) * sin


def _rms(x, g):
    return x * lax.rsqrt(jnp.mean(x * x, axis=-1, keepdims=True) + RMS_EPS) * g


def _prep_kernel(cq_ref, ckv_ref, kr_ref, rq_ref, rk_ref, gq_ref, gk_ref, gv_ref,
                 qn_ref, kvn_ref, wq_ref, wk_ref, wv_ref, gqn_ref, gkn_ref,
                 ca_ref, sa_ref, cb_ref, sb_ref, cc_ref, sc_ref,
                 qa_ref, ka_ref, va_ref, qb_ref, kb_ref, qc_ref, kc_ref, vc_ref):
    cqn = _rms(cq_ref[...], qn_ref[...]).astype(MXU)
    q = jnp.dot(cqn, wq_ref[...], preferred_element_type=F32)
    kvn = _rms(ckv_ref[...], kvn_ref[...]).astype(MXU)
    k = jnp.dot(kvn, wk_ref[...], preferred_element_type=F32)
    v = jnp.dot(kvn, wv_ref[...], preferred_element_type=F32)
    for h in range(MLA_HEADS):
        sl = slice(h * LANES, (h + 1) * LANES)
        va_ref[sl, :] = v[:, sl].T.astype(va_ref.dtype)
    ca, sa = ca_ref[...], sa_ref[...]
    kr = _rope(kr_ref[...], ca, sa, MLA_ROPE // 4)
    for h in range(MLA_HEADS):
        sl = slice(h * LANES, (h + 1) * LANES)
        qa_ref[:, sl] = (_rope(q[:, sl], ca, sa, MLA_ROPE // 4) * (MLA_SCALE * LOG2E)).astype(qa_ref.dtype)
        ka_ref[:, sl] = (k[:, sl] + kr).astype(ka_ref.dtype)
    for half in range(RET_HEADS * RET_DK // LANES):
        sl = slice(half * LANES, (half + 1) * LANES)
        cb, sb = cb_ref[:, sl], sb_ref[:, sl]
        qb_ref[:, sl] = _rope(rq_ref[:, sl], cb, sb, RET_DK // 4)
        kb_ref[:, sl] = _rope(rk_ref[:, sl], cb, sb, RET_DK // 4) * RET_K_SCALE
    cc, sc = cc_ref[...], sc_ref[...]
    for h in range(GQA_HEADS):
        sl = slice(h * LANES, (h + 1) * LANES)
        qc_ref[:, sl] = (_rope(_rms(gq_ref[:, sl], gqn_ref[...]), cc, sc, GQA_DH // 4)
                         * (GQA_SCALE * LOG2E)).astype(qc_ref.dtype)
    for h in range(GQA_KV_HEADS):
        sl = slice(h * LANES, (h + 1) * LANES)
        kc_ref[:, sl] = _rope(_rms(gk_ref[:, sl], gkn_ref[...]), cc, sc, GQA_DH // 4).astype(kc_ref.dtype)
    for h in range(GQA_KV_HEADS):
        sl = slice(h * LANES, (h + 1) * LANES)
        vc_ref[sl, :] = gv_ref[:, sl].T.astype(vc_ref.dtype)


def _prep(P, seg, wts, tabs, tab_of_tile, tm):
    n = P.shape[0]

    def pcol(name):
        off, w = seg[name]
        return pl.BlockSpec((tm, w), lambda i, o=off // w: (i, o))

    def const(a):
        return pl.BlockSpec(a.shape, lambda i: (0,) * a.ndim)

    def tab(a):
        return pl.BlockSpec((tm, a.shape[1]), lambda i: (tab_of_tile(i), 0))

    def rows(w):
        return pl.BlockSpec((tm, w), lambda i: (i, 0))

    consts = [wts["qn"], wts["kvn"], wts["wq"], wts["wk"], wts["wv"], wts["gqn"], wts["gkn"]]
    hq, hk = GQA_HEADS * LANES, GQA_KV_HEADS * LANES
    ha = MLA_HEADS * LANES
    hb = RET_HEADS * RET_DK
    outs = [(ha, MXU), (ha, MXU), (ha, MXU), (hb, F32), (hb, F32), (hq, MXU), (hk, MXU), (hk, MXU)]
    transposed = (2, 7)
    return pl.pallas_call(
        _prep_kernel,
        name="prep",
        grid=(n // tm,),
        in_specs=[pcol(s) for s in ("cq", "ckv", "kr", "rq", "rk", "gq", "gk", "gv")]
        + [const(a) for a in consts] + [tab(a) for a in tabs],
        out_specs=[pl.BlockSpec((w, tm), lambda i: (0, i)) if k in transposed else rows(w)
                   for k, (w, _) in enumerate(outs)],
        out_shape=[jax.ShapeDtypeStruct((w, n) if k in transposed else (n, w), dt)
                   for k, (w, dt) in enumerate(outs)],
        compiler_params=_params(("parallel",)),
    )(*([P] * 8), *consts, *tabs)


def _attn_kernel(*refs, group, tk, has_lat):
    if has_lat:
        q_ref, kc_ref, vc_ref, kl_ref, vl_ref, o_ref, acc_scr, s0_scr, s1_scr = refs
    else:
        q_ref, kc_ref, vc_ref, o_ref, acc_scr, s0_scr, s1_scr = refs
    s_scr = (s0_scr, s1_scr)
    tq = q_ref.shape[0]
    if group > 1:
        q = jnp.concatenate([q_ref[:, g * LANES:(g + 1) * LANES] for g in range(group)], axis=0)
    else:
        q = q_ref[...]
    rows = q.shape[0]
    chunks = [(kc_ref, vc_ref, 0, kc_ref.shape[0])]
    if has_lat:
        chunks += [(kl_ref, vl_ref, c * tk, tk) for c in range(kl_ref.shape[0] // tk)]

    def scores(c):
        k_ref, _, off, size = chunks[c]
        s_scr[c % 2][:size, :] = lax.dot_general(k_ref[off:off + size, :], q, (((1,), (1,)), ((), ())),
                                                 preferred_element_type=F32)

    def softmax_values(c, m, l):
        _, vt_ref, off, size = chunks[c]
        s = s_scr[c % 2][:size, :]
        m_new = jnp.maximum(m, jnp.max(s, axis=0, keepdims=True))
        a = jnp.exp2(m - m_new)
        p = jnp.exp2(s - m_new)
        acc_scr[...] = a * acc_scr[...] + jnp.dot(vt_ref[:, off:off + size], p.astype(MXU),
                                                  preferred_element_type=F32)
        return m_new, a * l + jnp.sum(p, axis=0, keepdims=True)

    acc_scr[...] = jnp.zeros_like(acc_scr)
    m, l = jnp.full((1, rows), NEG, F32), jnp.zeros((1, rows), F32)
    scores(0)
    for c in range(len(chunks)):
        if c + 1 < len(chunks):
            scores(c + 1)
        m, l = softmax_values(c, m, l)
    o = acc_scr[...] * (1.0 / l)
    for g in range(group):
        o_ref[:, g * LANES:(g + 1) * LANES] = o[:, g * tq:(g + 1) * tq].T.astype(o_ref.dtype)


def _attention(q, kc, vct, kl, vlt, batch, group, rows_q):
    n, hq = q.shape
    hk = hq // (group * LANES)
    ctx_len = kc.shape[0] // batch
    has_lat = kl is not None
    tq = _tile(rows_q, 512 // group)
    nq = rows_q // tq
    in_specs = [pl.BlockSpec((tq, group * LANES), lambda b, h, i: (b * nq + i, h)),
                pl.BlockSpec((ctx_len, LANES), lambda b, h, i: (b, h)),
                pl.BlockSpec((LANES, ctx_len), lambda b, h, i: (h, b))]
    args = [q, kc, vct]
    tk = 512
    if has_lat:
        seq = kl.shape[0] // batch
        tk = _tile(seq, 512)
        in_specs += [pl.BlockSpec((seq, LANES), lambda b, h, i: (b, h)),
                     pl.BlockSpec((LANES, seq), lambda b, h, i: (h, b))]
        args += [kl, vlt]
    return pl.pallas_call(
        functools.partial(_attn_kernel, group=group, tk=tk, has_lat=has_lat),
        name=f"attn_g{group}_{'lat' if has_lat else 'ctx'}",
        grid=(batch, hk, nq),
        in_specs=in_specs,
        out_specs=pl.BlockSpec((tq, group * LANES), lambda b, h, i: (b * nq + i, h)),
        out_shape=jax.ShapeDtypeStruct((n, hq), MXU),
        scratch_shapes=[pltpu.VMEM((LANES, group * tq), F32)]
        + [pltpu.VMEM((max(tk, ctx_len), group * tq), F32)] * 2,
        compiler_params=_params(("parallel", "parallel", "arbitrary")),
    )(*args)


def _ret_kernel(q_ref, k_ref, v_ref, g_ref, dm_ref, xi_ref, zeta_ref, gc_ref, r0_ref,
                o_ref, rT_ref, r_scr, yf_scr):
    d = pl.program_id(1)
    j = pl.program_id(2)
    nch = pl.num_programs(2)
    C = RET_CHUNK

    @pl.when(j == 0)
    def _():
        r_scr[...] = r0_ref[...]

    q = q_ref[...]
    k = k_ref[...]
    vb = v_ref[...].astype(MXU)
    qb = q.astype(MXU)
    qx = (q * xi_ref[...]).astype(MXU)
    kT = k.T
    kTb = kT.astype(MXU)
    kzT = (kT * zeta_ref[...]).astype(MXU)
    ys = []
    for h in range(RET_HEADS):
        ks = slice(h * RET_DK, (h + 1) * RET_DK)
        vs = slice(h * RET_DV, (h + 1) * RET_DV)
        r_h = r_scr[ks, :]
        s = jnp.dot(qb[:, ks], kTb[ks, :], preferred_element_type=F32) * dm_ref[h]
        y = (jnp.dot(s.astype(MXU), vb[:, vs], preferred_element_type=F32)
             + jnp.dot(qx[:, ks], r_h.astype(MXU), preferred_element_type=F32))
        r_scr[ks, :] = gc_ref[ks, :] * r_h + jnp.dot(kzT[ks, :], vb[:, vs], preferred_element_type=F32)
        ys.append(y)

    chunk = jnp.where(d == 0, j, nch - 1 - j)
    row0 = pl.multiple_of(chunk * C, C)

    @pl.when(d == 0)
    def _():
        for h in range(RET_HEADS):
            yf_scr[pl.ds(row0, C), h * RET_DV:(h + 1) * RET_DV] = ys[h]

    @pl.when(d == 1)
    def _():
        for h in range(RET_HEADS):
            vs = slice(h * RET_DV, (h + 1) * RET_DV)
            y = ys[h] + yf_scr[pl.ds(row0, C), vs]
            mu = jnp.mean(y, axis=-1, keepdims=True)
            yc = y - mu
            yn = yc * lax.rsqrt(jnp.mean(yc * yc, axis=-1, keepdims=True) + NORM_EPS)
            g = g_ref[:, vs]
            o_ref[:, vs] = (g * jax.nn.sigmoid(g) * yn).astype(o_ref.dtype)

    @pl.when(j == nch - 1)
    def _():
        rT_ref[...] = r_scr[...]


def _retention(qb, kb, P, seg, dec, r0, batch):
    n = qb.shape[0]
    C = RET_CHUNK
    nch = n // batch // C
    hk, hv = RET_HEADS * RET_DK, RET_HEADS * RET_DV
    cv, cg = seg["rv"][0] // hv, seg["rg"][0] // hv

    def chunk_of(d, j):
        return jnp.where(d == 0, j, nch - 1 - j)

    def rows(b, d, j):
        return b * nch + chunk_of(d, j)

    def out_rows(b, d, j):
        return b * nch + jnp.where(d == 0, nch - 1, nch - 1 - j)

    return pl.pallas_call(
        _ret_kernel,
        name="retention",
        grid=(batch, 2, nch),
        in_specs=[pl.BlockSpec((C, hk), lambda b, d, j: (rows(b, d, j), 0)),
                  pl.BlockSpec((C, hk), lambda b, d, j: (rows(b, d, j), 0)),
                  pl.BlockSpec((C, hv), lambda b, d, j: (rows(b, d, j), cv)),
                  pl.BlockSpec((C, hv), lambda b, d, j: (out_rows(b, d, j), cg)),
                  pl.BlockSpec((None, RET_HEADS, C, C), lambda b, d, j: (d, 0, 0, 0)),
                  pl.BlockSpec((None, C, hk), lambda b, d, j: (d, 0, 0)),
                  pl.BlockSpec((None, hk, C), lambda b, d, j: (d, 0, 0)),
                  pl.BlockSpec((None, hk, RET_DV), lambda b, d, j: (d, 0, 0)),
                  pl.BlockSpec((None, None, hk, RET_DV), lambda b, d, j: (b, d, 0, 0))],
        out_specs=[pl.BlockSpec((C, hv), lambda b, d, j: (out_rows(b, d, j), 0)),
                   pl.BlockSpec((None, None, hk, RET_DV), lambda b, d, j: (b, d, 0, 0))],
        out_shape=[jax.ShapeDtypeStruct((n, hv), MXU),
                   jax.ShapeDtypeStruct((batch, 2, hk, RET_DV), F32)],
        scratch_shapes=[pltpu.VMEM((hk, RET_DV), F32), pltpu.VMEM((nch * C, hv), F32)],
        compiler_params=_params(("parallel", "arbitrary", "arbitrary")),
    )(qb, kb, P, P, dec["dm"], dec["xi"], dec["zetaT"], dec["gc"], r0)


def _decay_tables(logit):
    C = RET_CHUNK
    log_g = jax.nn.log_sigmoid(logit.astype(F32))
    i = jnp.arange(C, dtype=F32)
    diff = i[:, None] - i[None, :]
    lg = log_g[:, :, None, None]
    fwd = jnp.where(diff >= 0, jnp.exp(lg[0] * jnp.maximum(diff, 0.0)), 0.0)
    bwd = jnp.where(diff <= 0, jnp.exp(lg[1] * jnp.maximum(-diff, 0.0)), 0.0)
    rep = lambda a: jnp.repeat(a, RET_DK, axis=0)
    xi_f = jnp.exp(log_g[0][:, None] * (i + 1.0))
    xi_b = jnp.exp(log_g[1][:, None] * (C - i))
    ze_f = jnp.exp(log_g[0][:, None] * (C - 1.0 - i))
    ze_b = jnp.exp(log_g[1][:, None] * i)
    gc = jnp.exp(log_g * C)
    return {
        "dm": jnp.stack([fwd, bwd]),
        "xi": jnp.stack([rep(xi_f).T, rep(xi_b).T]),
        "zetaT": jnp.stack([rep(ze_f), rep(ze_b)]),
        "gc": jnp.broadcast_to(jnp.repeat(gc, RET_DK, axis=1)[:, :, None],
                               (2, RET_HEADS * RET_DK, RET_DV)),
    }


def _merge_kernel(oa_ref, ob_ref, oc_ref, ga_ref, gb_ref, gc_ref, wa_ref, wb_ref, wc_ref, m_ref):
    m = (jax.nn.sigmoid(ga_ref[...]) * jnp.dot(oa_ref[...], wa_ref[...], preferred_element_type=F32)
         + jax.nn.sigmoid(gb_ref[...]) * jnp.dot(ob_ref[...], wb_ref[...], preferred_element_type=F32)
         + jax.nn.sigmoid(gc_ref[...]) * jnp.dot(oc_ref[...], wc_ref[...], preferred_element_type=F32))
    m_ref[...] = m.astype(m_ref.dtype)


def _merge(oa, ob, oc, P, seg, wa, wb, wc, tm):
    n = oa.shape[0]
    D = wa.shape[1]
    tn = _tile(D, 512)
    nb = D // tn

    def rows(a):
        return pl.BlockSpec((tm, a.shape[1]), lambda i, j: (i, 0))

    def gate(name):
        return pl.BlockSpec((tm, tn), lambda i, j, o=seg[name][0] // tn: (i, o + j))

    def wcol(a):
        return pl.BlockSpec((a.shape[0], tn), lambda i, j: (0, j))

    return pl.pallas_call(
        _merge_kernel,
        name="merge",
        grid=(n // tm, nb),
        in_specs=[rows(oa), rows(ob), rows(oc), gate("ga"), gate("gb"), gate("gc"),
                  wcol(wa), wcol(wb), wcol(wc)],
        out_specs=pl.BlockSpec((tm, tn), lambda i, j: (i, j)),
        out_shape=jax.ShapeDtypeStruct((n, D), MXU),
        compiler_params=_params(("parallel", "arbitrary")),
    )(oa, ob, oc, P, P, P, wa, wb, wc)


def _layer_norm(y, g, b):
    mu = jnp.mean(y, axis=-1, keepdims=True)
    yc = y - mu
    return yc * lax.rsqrt(jnp.mean(yc * yc, axis=-1, keepdims=True) + NORM_EPS) * g + b


def _outln_kernel(m_ref, w_ref, x_ref, g1_ref, lg_ref, lb_ref, o_ref, *, alpha):
    o = jnp.dot(m_ref[...], w_ref[...], preferred_element_type=F32)
    o_ref[...] = _layer_norm(alpha * x_ref[...] + g1_ref[...] * o, lg_ref[...], lb_ref[...])


def _outln(m, w, xs, mod3, lg, lb, row_of_tile, alpha, tm):
    n, D = xs.shape
    vec = pl.BlockSpec((1, D), lambda i: (0, 0))
    return pl.pallas_call(
        functools.partial(_outln_kernel, alpha=alpha),
        name="outln",
        grid=(n // tm,),
        in_specs=[pl.BlockSpec((tm, D), lambda i: (i, 0)),
                  pl.BlockSpec((D, D), lambda i: (0, 0)),
                  pl.BlockSpec((tm, D), lambda i: (i, 0)),
                  pl.BlockSpec((None, 1, D), lambda i: (row_of_tile(i, tm), 0, 2)),
                  vec, vec],
        out_specs=pl.BlockSpec((tm, D), lambda i: (i, 0)),
        out_shape=jax.ShapeDtypeStruct((n, D), F32),
        compiler_params=_params(("parallel",)),
    )(m, w, xs, mod3, lg, lb)


def _bitonic_merge(xs):
    n = len(xs)
    if n == 1:
        return xs
    half = n // 2
    hi = [jnp.maximum(xs[i], xs[i + half]) for i in range(half)]
    lo = [jnp.minimum(xs[i], xs[i + half]) for i in range(half)]
    return _bitonic_merge(hi) + _bitonic_merge(lo)


def _sort_desc(xs):
    n = len(xs)
    if n == 1:
        return xs
    return _bitonic_merge(_sort_desc(xs[:n // 2]) + _sort_desc(xs[n // 2:])[::-1])


def _top16_of(groups):
    while len(groups) > 1:
        nxt = []
        for a, b in zip(groups[0::2], groups[1::2]):
            nxt.append(_bitonic_merge([jnp.maximum(a[i], b[PEER_TOPK - 1 - i]) for i in range(PEER_TOPK)]))
        groups = nxt
    return groups[0]


def _top16(xs):
    return _top16_of([_sort_desc(xs[g:g + PEER_TOPK]) for g in range(0, len(xs), PEER_TOPK)])


def _router_kernel(x_ref, sh_ref, sc_ref, wq_ref, k1_ref, k2_ref, k2h_ref,
                   ht_ref, thr_ref, e1_ref, s2_ref, e2_ref, s1_scr, s2_scr):
    K, H = PEER_N_KEYS, PEER_HEADS
    tm = x_ref.shape[0]
    h = x_ref[...] * (1.0 + sc_ref[...]) + sh_ref[...]
    hT = h.T.astype(MXU)
    ht_ref[...] = hT
    qT = jnp.dot(wq_ref[...], hT, preferred_element_type=F32).astype(MXU)
    s1_scr[...] = jnp.dot(k1_ref[...], qT, preferred_element_type=F32).reshape(K, H, tm)
    s2_scr[...] = jnp.dot(k2_ref[...], qT, preferred_element_type=F32).reshape(K, H, tm)
    s2_ref[...] = jnp.dot(k2h_ref[...], qT, preferred_element_type=F32).reshape(H, K, tm)

    def lane_block(c, _):
        ls = pl.ds(pl.multiple_of(c * LANES, LANES), LANES)
        s1 = s1_scr[:, :, ls]
        v1 = _top16([s1[k] for k in range(K)])
        v2 = _top16([s2_scr[k, :, ls] for k in range(K)])
        pairs = [(a, b) for a in range(PEER_TOPK) for b in range(PEER_TOPK)
                 if (a + 1) * (b + 1) <= PEER_TOPK]
        cand = [v1[a] + v2[b] for a, b in pairs]
        pad = [jnp.full_like(cand[0], -jnp.inf)] * (-len(cand) % PEER_TOPK)
        t = _top16(cand + pad)[-1]
        ex1 = [jnp.exp(v - v1[0]) for v in v1]
        ex2 = [jnp.exp(v - v2[0]) for v in v2]
        z = jnp.zeros_like(t)
        for (a, b), cv in zip(pairs, cand):
            z = z + jnp.where(cv >= t, ex1[a] * ex2[b], 0.0)
        inv_z = 1.0 / z
        thr = jnp.full(s1.shape, jnp.inf, F32)
        for b in range(PEER_TOPK):
            thr = jnp.where(s1 + v2[b][None] >= t[None], v2[b][None], thr)
        thr_ref[:, :, ls] = thr
        e1_ref[:, :, ls] = jnp.where(s1 >= v1[-1][None], jnp.exp(s1 - v1[0][None]), 0.0)
        for hh in range(H):
            s2h = s2_ref[hh, :, ls]
            e2_ref[hh, :, ls] = jnp.where(
                s2h >= v2[-1][hh:hh + 1], jnp.exp(s2h - v2[0][hh:hh + 1]) * inv_z[hh:hh + 1], 0.0)
        return 0

    lax.fori_loop(0, tm // LANES, lane_block, 0)


def _router(x1, mod3, wts, row_of_tile, tm):
    n, D = x1.shape
    K, H = PEER_N_KEYS, PEER_HEADS
    HQ = H * PEER_DQ

    def const(a):
        return pl.BlockSpec(a.shape, lambda i: (0,) * a.ndim)

    kh = pl.BlockSpec((K, H, tm), lambda i: (0, 0, i))
    hk = pl.BlockSpec((H, K, tm), lambda i: (0, 0, i))
    return pl.pallas_call(
        _router_kernel,
        name="router",
        grid=(n // tm,),
        in_specs=[pl.BlockSpec((tm, D), lambda i: (i, 0)),
                  pl.BlockSpec((None, 1, D), lambda i: (row_of_tile(i, tm), 0, 3)),
                  pl.BlockSpec((None, 1, D), lambda i: (row_of_tile(i, tm), 0, 4)),
                  const(wts["wqT"]), const(wts["k1kh"]), const(wts["k2kh"]), const(wts["k2hk"])],
        out_specs=[pl.BlockSpec((D, tm), lambda i: (0, i)), kh, kh, hk, hk],
        out_shape=[jax.ShapeDtypeStruct((D, n), MXU),
                   jax.ShapeDtypeStruct((K, H, n), F32), jax.ShapeDtypeStruct((K, H, n), F32),
                   jax.ShapeDtypeStruct((H, K, n), F32), jax.ShapeDtypeStruct((H, K, n), F32)],
        scratch_shapes=[pltpu.VMEM((K, H, tm), F32), pltpu.VMEM((K, H, tm), F32)],
        compiler_params=_params(("parallel",)),
    )(x1, mod3, mod3, wts["wqT"], wts["k1kh"], wts["k2kh"], wts["k2hk"])


PEER_ROWS = 32
PEER_TE = 512


def _peer_kernel(go_ref, ua_ref, ub_ref, u0_ref, vta_ref, vtb_ref, vtl_ref, ht_ref,
                 thr_ref, e1_ref, s2_ref, e2_ref, x_ref, g2_ref, lg_ref, lb_ref, o_ref,
                 acc, sc0, sc1, a0, a1, *, alpha):
    e = pl.program_id(1)
    last = pl.num_programs(1) - 1
    te, tm = sc0.shape
    K = PEER_N_KEYS
    nsub = te // K
    go1, go2 = go_ref[0] != 0, go_ref[1] != 0

    def scores(u_tile_ref, sc):
        sc[...] = jnp.dot(u_tile_ref[...], ht_ref[...], preferred_element_type=F32)

    def gates(sc, a, tile):
        for c in range(tm // LANES):
            ls = slice(c * LANES, (c + 1) * LANES)
            thr = [thr_ref[tile * nsub + sub, :, ls] for sub in range(nsub)]
            e1r = [e1_ref[tile * nsub + sub, :, ls] for sub in range(nsub)]
            for rb in range(K // PEER_ROWS):
                rs = slice(rb * PEER_ROWS, (rb + 1) * PEER_ROWS)
                g = [None] * nsub
                for h in range(PEER_HEADS):
                    s2p, e2p = s2_ref[h, rs, ls], e2_ref[h, rs, ls]
                    for sub in range(nsub):
                        w = jnp.where(s2p >= thr[sub][h:h + 1], e2p, 0.0) * e1r[sub][h:h + 1]
                        g[sub] = w if g[sub] is None else g[sub] + w
                for sub in range(nsub):
                    xs = slice(sub * K + rb * PEER_ROWS, sub * K + (rb + 1) * PEER_ROWS)
                    x = sc[xs, ls]
                    act = 0.5 * x * (1.0 + lax.erf(x * SQRT_HALF))
                    a[xs, ls] = (act * g[sub]).astype(a.dtype)

    def values(vt_tile_ref, a):
        acc[...] += jnp.dot(vt_tile_ref[...], a[...], preferred_element_type=F32)

    @pl.when(e == 0)
    def _():
        acc[...] = jnp.zeros_like(acc)
        a1[...] = jnp.zeros_like(a1)
        scores(u0_ref, sc0)

    @pl.when(go1)
    def _():
        scores(ub_ref, sc1)
        gates(sc0, a0, 2 * e)
        values(vtb_ref, a1)

    @pl.when(go2)
    def _():
        scores(ua_ref, sc0)
        gates(sc1, a1, 2 * e + 1)
        values(vta_ref, a0)

    @pl.when(e == last)
    def _():
        values(vtl_ref, a1)
        y = alpha * x_ref[...] + g2_ref[...] * acc[...].T
        o_ref[...] = _layer_norm(y, lg_ref[...], lb_ref[...])


def _peer(x1, routed, u, vt, mod3, lg, lb, row_of_tile, alpha, tm):
    n, D = x1.shape
    K, H = PEER_N_KEYS, PEER_HEADS
    te = PEER_TE
    nt = u.shape[0] // te
    ht, thr, e1, s2, e2 = routed
    once = dict(pipeline_mode=pl.Buffered(1))
    kh = pl.BlockSpec((K, H, tm), lambda i, e: (0, 0, i), **once)
    hk = pl.BlockSpec((H, K, tm), lambda i, e: (0, 0, i), **once)
    vec = pl.BlockSpec((1, D), lambda i, e: (0, 0))
    return pl.pallas_call(
        functools.partial(_peer_kernel, alpha=alpha),
        name="peer",
        grid=(n // tm, nt // 2),
        in_specs=[pl.BlockSpec(memory_space=pltpu.SMEM),
                  pl.BlockSpec((te, D), lambda i, e: (jnp.minimum(2 * e + 2, nt - 1), 0)),
                  pl.BlockSpec((te, D), lambda i, e: (2 * e + 1, 0)),
                  pl.BlockSpec((te, D), lambda i, e: (0, 0), **once),
                  pl.BlockSpec((D, te), lambda i, e: (0, 2 * e)),
                  pl.BlockSpec((D, te), lambda i, e: (0, jnp.maximum(2 * e - 1, 0))),
                  pl.BlockSpec((D, te), lambda i, e: (0, nt - 1), **once),
                  pl.BlockSpec((D, tm), lambda i, e: (0, i), **once),
                  kh, kh, hk, hk,
                  pl.BlockSpec((tm, D), lambda i, e: (i, 0), **once),
                  pl.BlockSpec((None, 1, D), lambda i, e: (row_of_tile(i, tm), 0, 5)),
                  vec, vec],
        out_specs=pl.BlockSpec((tm, D), lambda i, e: (i, 0)),
        out_shape=jax.ShapeDtypeStruct((n, D), F32),
        scratch_shapes=[pltpu.VMEM((D, tm), F32), pltpu.VMEM((te, tm), F32), pltpu.VMEM((te, tm), F32),
                        pltpu.VMEM((te, tm), MXU), pltpu.VMEM((te, tm), MXU)],
        compiler_params=_params(("arbitrary", "arbitrary"), 58),
    )(jnp.ones((2,), jnp.int32), u, u, u, vt, vt, vt, ht, thr, e1, s2, e2, x1, mod3, lg, lb)


def _pad_heads(w, heads, width, lo=0):
    lead = w.shape[:-1]
    w = w.reshape(*lead, heads, width)
    w = jnp.pad(w, [(0, 0)] * len(lead) + [(0, 0), (lo, LANES - lo - width)])
    return w.reshape(*lead, heads * LANES)


def _layer_weights(l, seg, total, w_in, mla_q_norm, mla_w_qup, mla_kv_norm, mla_w_kvup, gqa_q_norm,
                   gqa_k_norm, w_br_a, w_br_b, w_br_c, w_out, peer_w_q, peer_k1, peer_k2, peer_u, peer_v):
    D = w_in.shape[1]
    widths = (MLA_Q_LORA, MLA_KV_LORA, MLA_ROPE, RET_HEADS * RET_DK, RET_HEADS * RET_DK,
              RET_HEADS * RET_DV, RET_HEADS * RET_DV, GQA_HEADS * GQA_DH, GQA_KV_HEADS * GQA_DH,
              GQA_KV_HEADS * GQA_DH, D, D, D)
    names = ("cq", "ckv", "kr", "rq", "rk", "rv", "rg", "gq", "gk", "gv", "ga", "gb", "gc")
    cols = dict(zip(names, jnp.split(w_in[l], np.cumsum(widths)[:-1].tolist(), axis=1)))
    lo = MLA_NOPE
    cols["kr"] = jnp.pad(cols["kr"], ((0, 0), (lo, LANES - lo - MLA_ROPE)))
    order = sorted(seg, key=lambda name: seg[name][0])
    used = seg[order[-1]][0] + seg[order[-1]][1]
    win = jnp.concatenate([cols[name] for name in order] + [jnp.zeros((D, total - used), F32)],
                          axis=1).astype(MXU)
    kv = mla_w_kvup[l].reshape(MLA_KV_LORA, MLA_HEADS, MLA_NOPE + MLA_V)
    eye = jnp.eye(PEER_HEADS, dtype=F32)
    half = PEER_DQ // 2
    k1p = jnp.pad(peer_k1[l], ((0, 0), (0, half)))
    k2p = jnp.pad(peer_k2[l], ((0, 0), (half, 0)))
    HQ = PEER_HEADS * PEER_DQ
    return {
        "win": win,
        "qn": mla_q_norm[l][None], "kvn": mla_kv_norm[l][None],
        "wq": _pad_heads(mla_w_qup[l], MLA_HEADS, MLA_NOPE + MLA_ROPE).astype(MXU),
        "wk": _pad_heads(kv[..., :MLA_NOPE].reshape(MLA_KV_LORA, -1), MLA_HEADS, MLA_NOPE).astype(MXU),
        "wv": _pad_heads(kv[..., MLA_NOPE:].reshape(MLA_KV_LORA, -1), MLA_HEADS, MLA_V).astype(MXU),
        "gqn": gqa_q_norm[l][None], "gkn": gqa_k_norm[l][None],
        "wa": _pad_heads(w_br_a[l].T, MLA_HEADS, MLA_V).T.astype(MXU),
        "wb": w_br_b[l].astype(MXU), "wc": w_br_c[l].astype(MXU), "wo": w_out[l].astype(MXU),
        "wqT": peer_w_q[l].T.astype(MXU),
        "k1kh": jnp.einsum("kd,hg->khgd", k1p, eye).reshape(HQ, HQ).astype(MXU),
        "k2kh": jnp.einsum("kd,hg->khgd", k2p, eye).reshape(HQ, HQ).astype(MXU),
        "k2hk": jnp.einsum("kd,hg->hkgd", k2p, eye).reshape(HQ, HQ).astype(MXU),
        "u": peer_u[l].astype(MXU), "vt": peer_v[l].T.astype(MXU),
    }


def _rope_tables(seq):
    t = jnp.arange(seq, dtype=jnp.int32)
    row, col = (t // GRID_W).astype(F32), (t % GRID_W).astype(F32)

    def tab(r):
        nf = r // 4
        inv = ROPE_BASE ** (-jnp.arange(nf, dtype=F32) / nf)
        ar, ac = row[:, None] * inv[None], col[:, None] * inv[None]
        cos = jnp.concatenate([jnp.cos(ar)] * 2 + [jnp.cos(ac)] * 2, axis=1)
        sin = jnp.concatenate([-jnp.sin(ar), jnp.sin(ar), -jnp.sin(ac), jnp.sin(ac)], axis=1)
        return cos, sin

    ca, sa = tab(MLA_ROPE)
    lo, hi = MLA_NOPE, LANES - MLA_NOPE - MLA_ROPE
    ca = jnp.pad(ca, ((0, 0), (lo, hi)), constant_values=1.0)
    sa = jnp.pad(sa, ((0, 0), (lo, hi)))
    cb, sb = tab(RET_DK)
    cb, sb = jnp.tile(cb, (1, RET_HEADS)), jnp.tile(sb, (1, RET_HEADS))
    cc, sc = tab(GQA_DH)
    return [ca, sa, cb, sb, cc, sc]


def kernel(x, c, ctx, c_ctx, w_mod, b_mod, w_in, mla_q_norm, mla_w_qup, mla_kv_norm, mla_w_kvup, ret_decay_logit, gqa_q_norm, gqa_k_norm, w_br_a, w_br_b, w_br_c, w_out, ln1_g, ln1_b, peer_w_q, peer_k1, peer_k2, peer_u, peer_v, ln2_g, ln2_b):
    B, S, D = x.shape
    CTX = ctx.shape[1]
    L = w_in.shape[0]
    alpha = float((2.0 * L) ** 0.25)
    seg, total = _layout(D)

    tm_l = _tile(S, 512)
    tm_c = _tile(B * CTX, 512)
    tpb = S // tm_l
    lat_row = lambda i, tm: (i * tm) // S
    ctx_row = lambda i, tm: B
    lat_tab = lambda i: i % tpb
    ctx_tab = lambda i: 0

    nrow = -(-(B + 1) // 8) * 8
    cvec = jnp.concatenate([c, c_ctx[None], jnp.zeros((nrow - B - 1, D), F32)], axis=0)
    mod = _modulation(cvec, w_mod, b_mod)

    tabs_l = _rope_tables(S)
    tabs_c = [jnp.ones((tm_c, a.shape[1]), F32) if k % 2 == 0 else jnp.zeros((tm_c, a.shape[1]), F32)
              for k, a in enumerate(tabs_l)]

    xl = x.reshape(B * S, D)
    xc = ctx.reshape(B * CTX, D)
    for l in range(L):
        last = l == L - 1
        W = _layer_weights(l, seg, total, w_in, mla_q_norm, mla_w_qup, mla_kv_norm, mla_w_kvup,
                           gqa_q_norm, gqa_k_norm, w_br_a, w_br_b, w_br_c, w_out,
                           peer_w_q, peer_k1, peer_k2, peer_u, peer_v)
        dec = _decay_tables(ret_decay_logit[l])
        mod3 = mod[l].reshape(nrow, 1, 6 * D)
        lg1, lb1, lg2, lb2 = ln1_g[l][None], ln1_b[l][None], ln2_g[l][None], ln2_b[l][None]

        Pl = _inproj(xl, mod3, W["win"], lat_row, _tile(S, 1024))
        Pc = _inproj(xc, mod3, W["win"], ctx_row, tm_c)
        qa, ka, va, qb, kb, qc, kc, vc = _prep(Pl, seg, W, tabs_l, lat_tab, tm_l)
        qa_c, ka_c, va_c, qb_c, kb_c, qc_c, kc_c, vc_c = _prep(Pc, seg, W, tabs_c, ctx_tab, tm_c)

        oa = _attention(qa, ka_c, va_c, ka, va, B, 1, S)
        oc = _attention(qc, kc_c, vc_c, kc, vc, B, GQA_GROUP, S)
        r0 = jnp.zeros((B, 2, RET_HEADS * RET_DK, RET_DV), F32)
        ob_c, r_ctx = _retention(qb_c, kb_c, Pc, seg, dec, r0, B)
        ob, _ = _retention(qb, kb, Pl, seg, dec, r_ctx, B)

        def tail(xs, oa, ob, oc, P, row_of_tile, tm):
            m = _merge(oa, ob, oc, P, seg, W["wa"], W["wb"], W["wc"], tm)
            x1 = _outln(m, W["wo"], xs, mod3, lg1, lb1, row_of_tile, alpha, _tile(tm, 256))
            routed = _router(x1, mod3, W, row_of_tile, tm)
            return _peer(x1, routed, W["u"], W["vt"], mod3, lg2, lb2, row_of_tile, alpha, tm)

        if not last:
            oa_c = _attention(qa_c, ka_c, va_c, None, None, B, 1, CTX)
            oc_c = _attention(qc_c, kc_c, vc_c, None, None, B, GQA_GROUP, CTX)
            xc = tail(xc, oa_c, ob_c, oc_c, Pc, ctx_row, tm_c)
        xl = tail(xl, oa, ob, oc, Pl, lat_row, tm_l)
    return xl.reshape(B, S, D)
```

```python
import functools

import numpy as np
import jax
import jax.numpy as jnp
from jax import lax
from jax.experimental import pallas as pl
from jax.experimental.pallas import tpu as pltpu

GRID_W = 64
ROPE_BASE = 10000.0
NORM_EPS = 1e-5
RMS_EPS = 1e-6
MLA_HEADS, MLA_Q_LORA, MLA_KV_LORA, MLA_NOPE, MLA_ROPE, MLA_V = 8, 512, 256, 64, 32, 64
MLA_SCALE = (MLA_NOPE + MLA_ROPE) ** -0.5
RET_HEADS, RET_DK, RET_DV, RET_CHUNK = 8, 32, 64, 128
RET_K_SCALE = RET_DK ** -0.5
GQA_HEADS, GQA_KV_HEADS, GQA_DH = 8, 2, 128
GQA_GROUP = GQA_HEADS // GQA_KV_HEADS
GQA_SCALE = GQA_DH ** -0.5
PEER_HEADS, PEER_DQ, PEER_N_KEYS, PEER_TOPK = 8, 128, 128, 16
PEER_N_EXPERTS = PEER_N_KEYS * PEER_N_KEYS

LANES = 128
MXU = jnp.bfloat16
F32 = jnp.float32
NEG = -1e30
SQRT_HALF = float(np.sqrt(0.5))
LOG2E = float(np.log2(np.e))

_SEG = (("ga", None), ("gb", None), ("gc", None), ("gq", GQA_HEADS * GQA_DH),
        ("cq", MLA_Q_LORA), ("rv", RET_HEADS * RET_DV), ("rg", RET_HEADS * RET_DV),
        ("ckv", MLA_KV_LORA), ("rq", RET_HEADS * RET_DK), ("rk", RET_HEADS * RET_DK),
        ("gk", GQA_KV_HEADS * GQA_DH), ("gv", GQA_KV_HEADS * GQA_DH), ("kr", LANES))


def _layout(d_model):
    seg, off = {}, 0
    widths = [(name, d_model if w is None else w) for name, w in _SEG]
    for name, w in sorted(widths, key=lambda nw: -nw[1]):
        assert off % w == 0, (name, off, w)
        seg[name] = (off, w)
        off += w
    total = -(-off // 512) * 512
    return seg, total


def _params(sem, vmem_mb=48, flags=None):
    return pltpu.CompilerParams(dimension_semantics=sem, vmem_limit_bytes=vmem_mb << 20, flags=flags)


def _tile(n, pref):
    t = min(n, pref)
    while n % t:
        t //= 2
    return t


def _mod_kernel(c_ref, w_ref, b_ref, o_ref):
    c = c_ref[...]
    a = (c * jax.nn.sigmoid(c)).astype(MXU)
    o_ref[...] = jnp.dot(a, w_ref[...].astype(MXU), preferred_element_type=F32) + b_ref[...]


def _modulation(cvec, w_mod, b_mod):
    L, D, N6 = w_mod.shape
    R = cvec.shape[0]
    tn = _tile(N6, 1024)
    return pl.pallas_call(
        _mod_kernel,
        name="mod",
        grid=(L, N6 // tn),
        in_specs=[pl.BlockSpec((R, D), lambda l, j: (0, 0)),
                  pl.BlockSpec((None, D, tn), lambda l, j: (l, 0, j)),
                  pl.BlockSpec((None, 1, tn), lambda l, j: (l, 0, j))],
        out_specs=pl.BlockSpec((None, R, tn), lambda l, j: (l, 0, j)),
        out_shape=jax.ShapeDtypeStruct((L, R, N6), F32),
        compiler_params=_params(("parallel", "parallel")),
    )(cvec, w_mod, b_mod.reshape(L, 1, N6))


def _inproj_kernel(x_ref, sh_ref, sc_ref, w_ref, o_ref, h_scr):
    @pl.when(pl.program_id(1) == 0)
    def _():
        h_scr[...] = (x_ref[...] * (1.0 + sc_ref[...]) + sh_ref[...]).astype(h_scr.dtype)

    o_ref[...] = jnp.dot(h_scr[...], w_ref[...], preferred_element_type=F32)


def _inproj(xs, mod3, w, row_of_tile, tm):
    n, D = xs.shape
    NP = w.shape[1]
    tn = 512
    return pl.pallas_call(
        _inproj_kernel,
        name="inproj",
        grid=(n // tm, NP // tn),
        in_specs=[pl.BlockSpec((tm, D), lambda i, j: (i, 0)),
                  pl.BlockSpec((None, 1, D), lambda i, j: (row_of_tile(i, tm), 0, 0)),
                  pl.BlockSpec((None, 1, D), lambda i, j: (row_of_tile(i, tm), 0, 1)),
                  pl.BlockSpec((D, tn), lambda i, j: (0, j))],
        out_specs=pl.BlockSpec((tm, tn), lambda i, j: (i, j)),
        out_shape=jax.ShapeDtypeStruct((n, NP), F32),
        scratch_shapes=[pltpu.VMEM((tm, D), MXU)],
        compiler_params=_params(("parallel", "arbitrary")),
    )(xs, mod3, mod3, w)


def _swap_pairs(x, nf):
    lane = lax.broadcasted_iota(jnp.int32, x.shape, 1)
    up = pltpu.roll(x, LANES - nf, 1)
    down = pltpu.roll(x, nf, 1)
    return jnp.where((lane & nf) == 0, up, down)


def _rope(x, cos, sin, nf):
    return x * cos + _swap_pairs(x, nf) * sin


def _rms(x, g):
    return x * lax.rsqrt(jnp.mean(x * x, axis=-1, keepdims=True) + RMS_EPS) * g


def _prep_kernel(cq_ref, ckv_ref, kr_ref, rq_ref, rk_ref, gq_ref, gk_ref, gv_ref,
                 qn_ref, kvn_ref, wq_ref, wk_ref, wv_ref, gqn_ref, gkn_ref,
                 ca_ref, sa_ref, cb_ref, sb_ref, cc_ref, sc_ref,
                 qa_ref, ka_ref, va_ref, qb_ref, kb_ref, qc_ref, kc_ref, vc_ref):
    cqn = _rms(cq_ref[...], qn_ref[...]).astype(MXU)
    q = jnp.dot(cqn, wq_ref[...], preferred_element_type=F32)
    kvn = _rms(ckv_ref[...], kvn_ref[...]).astype(MXU)
    k = jnp.dot(kvn, wk_ref[...], preferred_element_type=F32)
    v = jnp.dot(kvn, wv_ref[...], preferred_element_type=F32)
    for h in range(MLA_HEADS):
        sl = slice(h * LANES, (h + 1) * LANES)
        va_ref[sl, :] = v[:, sl].T.astype(va_ref.dtype)
    ca, sa = ca_ref[...], sa_ref[...]
    kr = _rope(kr_ref[...], ca, sa, MLA_ROPE // 4)
    for h in range(MLA_HEADS):
        sl = slice(h * LANES, (h + 1) * LANES)
        qa_ref[:, sl] = (_rope(q[:, sl], ca, sa, MLA_ROPE // 4) * (MLA_SCALE * LOG2E)).astype(qa_ref.dtype)
        ka_ref[:, sl] = (k[:, sl] + kr).astype(ka_ref.dtype)
    for half in range(RET_HEADS * RET_DK // LANES):
        sl = slice(half * LANES, (half + 1) * LANES)
        cb, sb = cb_ref[:, sl], sb_ref[:, sl]
        qb_ref[:, sl] = _rope(rq_ref[:, sl], cb, sb, RET_DK // 4)
        kb_ref[:, sl] = _rope(rk_ref[:, sl], cb, sb, RET_DK // 4) * RET_K_SCALE
    cc, sc = cc_ref[...], sc_ref[...]
    for h in range(GQA_HEADS):
        sl = slice(h * LANES, (h + 1) * LANES)
        qc_ref[:, sl] = (_rope(_rms(gq_ref[:, sl], gqn_ref[...]), cc, sc, GQA_DH // 4)
                         * (GQA_SCALE * LOG2E)).astype(qc_ref.dtype)
    for h in range(GQA_KV_HEADS):
        sl = slice(h * LANES, (h + 1) * LANES)
        kc_ref[:, sl] = _rope(_rms(gk_ref[:, sl], gkn_ref[...]), cc, sc, GQA_DH // 4).astype(kc_ref.dtype)
    for h in range(GQA_KV_HEADS):
        sl = slice(h * LANES, (h + 1) * LANES)
        vc_ref[sl, :] = gv_ref[:, sl].T.astype(vc_ref.dtype)


def _prep(P, seg, wts, tabs, tab_of_tile, tm):
    n = P.shape[0]

    def pcol(name):
        off, w = seg[name]
        return pl.BlockSpec((tm, w), lambda i, o=off // w: (i, o))

    def const(a):
        return pl.BlockSpec(a.shape, lambda i: (0,) * a.ndim)

    def tab(a):
        return pl.BlockSpec((tm, a.shape[1]), lambda i: (tab_of_tile(i), 0))

    def rows(w):
        return pl.BlockSpec((tm, w), lambda i: (i, 0))

    consts = [wts["qn"], wts["kvn"], wts["wq"], wts["wk"], wts["wv"], wts["gqn"], wts["gkn"]]
    hq, hk = GQA_HEADS * LANES, GQA_KV_HEADS * LANES
    ha = MLA_HEADS * LANES
    hb = RET_HEADS * RET_DK
    outs = [(ha, MXU), (ha, MXU), (ha, MXU), (hb, F32), (hb, F32), (hq, MXU), (hk, MXU), (hk, MXU)]
    transposed = (2, 7)
    return pl.pallas_call(
        _prep_kernel,
        name="prep",
        grid=(n // tm,),
        in_specs=[pcol(s) for s in ("cq", "ckv", "kr", "rq", "rk", "gq", "gk", "gv")]
        + [const(a) for a in consts] + [tab(a) for a in tabs],
        out_specs=[pl.BlockSpec((w, tm), lambda i: (0, i)) if k in transposed else rows(w)
                   for k, (w, _) in enumerate(outs)],
        out_shape=[jax.ShapeDtypeStruct((w, n) if k in transposed else (n, w), dt)
                   for k, (w, dt) in enumerate(outs)],
        compiler_params=_params(("parallel",)),
    )(*([P] * 8), *consts, *tabs)


def _attn_kernel(*refs, group, tk, has_lat):
    if has_lat:
        q_ref, kc_ref, vc_ref, kl_ref, vl_ref, o_ref, acc_scr, s0_scr, s1_scr = refs
    else:
        q_ref, kc_ref, vc_ref, o_ref, acc_scr, s0_scr, s1_scr = refs
    s_scr = (s0_scr, s1_scr)
    tq = q_ref.shape[0]
    if group > 1:
        q = jnp.concatenate([q_ref[:, g * LANES:(g + 1) * LANES] for g in range(group)], axis=0)
    else:
        q = q_ref[...]
    rows = q.shape[0]
    chunks = [(kc_ref, vc_ref, 0, kc_ref.shape[0])]
    if has_lat:
        chunks += [(kl_ref, vl_ref, c * tk, tk) for c in range(kl_ref.shape[0] // tk)]

    def scores(c):
        k_ref, _, off, size = chunks[c]
        s_scr[c % 2][:size, :] = lax.dot_general(k_ref[off:off + size, :], q, (((1,), (1,)), ((), ())),
                                                 preferred_element_type=F32)

    def softmax_values(c, m, l):
        _, vt_ref, off, size = chunks[c]
        s = s_scr[c % 2][:size, :]
        m_new = jnp.maximum(m, jnp.max(s, axis=0, keepdims=True))
        a = jnp.exp2(m - m_new)
        p = jnp.exp2(s - m_new)
        acc_scr[...] = a * acc_scr[...] + jnp.dot(vt_ref[:, off:off + size], p.astype(MXU),
                                                  preferred_element_type=F32)
        return m_new, a * l + jnp.sum(p, axis=0, keepdims=True)

    acc_scr[...] = jnp.zeros_like(acc_scr)
    m, l = jnp.full((1, rows), NEG, F32), jnp.zeros((1, rows), F32)
    scores(0)
    for c in range(len(chunks)):
        if c + 1 < len(chunks):
            scores(c + 1)
        m, l = softmax_values(c, m, l)
    o = acc_scr[...] * (1.0 / l)
    for g in range(group):
        o_ref[:, g * LANES:(g + 1) * LANES] = o[:, g * tq:(g + 1) * tq].T.astype(o_ref.dtype)


def _attention(q, kc, vct, kl, vlt, batch, group, rows_q):
    n, hq = q.shape
    hk = hq // (group * LANES)
    ctx_len = kc.shape[0] // batch
    has_lat = kl is not None
    tq = _tile(rows_q, 512 // group)
    nq = rows_q // tq
    in_specs = [pl.BlockSpec((tq, group * LANES), lambda b, h, i: (b * nq + i, h)),
                pl.BlockSpec((ctx_len, LANES), lambda b, h, i: (b, h)),
                pl.BlockSpec((LANES, ctx_len), lambda b, h, i: (h, b))]
    args = [q, kc, vct]
    tk = 512
    if has_lat:
        seq = kl.shape[0] // batch
        tk = _tile(seq, 512)
        in_specs += [pl.BlockSpec((seq, LANES), lambda b, h, i: (b, h)),
                     pl.BlockSpec((LANES, seq), lambda b, h, i: (h, b))]
        args += [kl, vlt]
    return pl.pallas_call(
        functools.partial(_attn_kernel, group=group, tk=tk, has_lat=has_lat),
        name=f"attn_g{group}_{'lat' if has_lat else 'ctx'}",
        grid=(batch, hk, nq),
        in_specs=in_specs,
        out_specs=pl.BlockSpec((tq, group * LANES), lambda b, h, i: (b * nq + i, h)),
        out_shape=jax.ShapeDtypeStruct((n, hq), MXU),
        scratch_shapes=[pltpu.VMEM((LANES, group * tq), F32)]
        + [pltpu.VMEM((max(tk, ctx_len), group * tq), F32)] * 2,
        compiler_params=_params(("parallel", "parallel", "arbitrary")),
    )(*args)


def _ret_kernel(q_ref, k_ref, v_ref, g_ref, dm_ref, xi_ref, zeta_ref, gc_ref, r0_ref,
                o_ref, rT_ref, r_scr, yf_scr):
    d = pl.program_id(1)
    j = pl.program_id(2)
    nch = pl.num_programs(2)
    C = RET_CHUNK

    @pl.when(j == 0)
    def _():
        r_scr[...] = r0_ref[...]

    q = q_ref[...]
    k = k_ref[...]
    vb = v_ref[...].astype(MXU)
    qb = q.astype(MXU)
    qx = (q * xi_ref[...]).astype(MXU)
    kT = k.T
    kTb = kT.astype(MXU)
    kzT = (kT * zeta_ref[...]).astype(MXU)
    ys = []
    for h in range(RET_HEADS):
        ks = slice(h * RET_DK, (h + 1) * RET_DK)
        vs = slice(h * RET_DV, (h + 1) * RET_DV)
        r_h = r_scr[ks, :]
        s = jnp.dot(qb[:, ks], kTb[ks, :], preferred_element_type=F32) * dm_ref[h]
        y = (jnp.dot(s.astype(MXU), vb[:, vs], preferred_element_type=F32)
             + jnp.dot(qx[:, ks], r_h.astype(MXU), preferred_element_type=F32))
        r_scr[ks, :] = gc_ref[ks, :] * r_h + jnp.dot(kzT[ks, :], vb[:, vs], preferred_element_type=F32)
        ys.append(y)

    chunk = jnp.where(d == 0, j, nch - 1 - j)
    row0 = pl.multiple_of(chunk * C, C)

    @pl.when(d == 0)
    def _():
        for h in range(RET_HEADS):
            yf_scr[pl.ds(row0, C), h * RET_DV:(h + 1) * RET_DV] = ys[h]

    @pl.when(d == 1)
    def _():
        for h in range(RET_HEADS):
            vs = slice(h * RET_DV, (h + 1) * RET_DV)
            y = ys[h] + yf_scr[pl.ds(row0, C), vs]
            mu = jnp.mean(y, axis=-1, keepdims=True)
            yc = y - mu
            yn = yc * lax.rsqrt(jnp.mean(yc * yc, axis=-1, keepdims=True) + NORM_EPS)
            g = g_ref[:, vs]
            o_ref[:, vs] = (g * jax.nn.sigmoid(g) * yn).astype(o_ref.dtype)

    @pl.when(j == nch - 1)
    def _():
        rT_ref[...] = r_scr[...]


def _retention(qb, kb, P, seg, dec, r0, batch):
    n = qb.shape[0]
    C = RET_CHUNK
    nch = n // batch // C
    hk, hv = RET_HEADS * RET_DK, RET_HEADS * RET_DV
    cv, cg = seg["rv"][0] // hv, seg["rg"][0] // hv

    def chunk_of(d, j):
        return jnp.where(d == 0, j, nch - 1 - j)

    def rows(b, d, j):
        return b * nch + chunk_of(d, j)

    def out_rows(b, d, j):
        return b * nch + jnp.where(d == 0, nch - 1, nch - 1 - j)

    return pl.pallas_call(
        _ret_kernel,
        name="retention",
        grid=(batch, 2, nch),
        in_specs=[pl.BlockSpec((C, hk), lambda b, d, j: (rows(b, d, j), 0)),
                  pl.BlockSpec((C, hk), lambda b, d, j: (rows(b, d, j), 0)),
                  pl.BlockSpec((C, hv), lambda b, d, j: (rows(b, d, j), cv)),
                  pl.BlockSpec((C, hv), lambda b, d, j: (out_rows(b, d, j), cg)),
                  pl.BlockSpec((None, RET_HEADS, C, C), lambda b, d, j: (d, 0, 0, 0)),
                  pl.BlockSpec((None, C, hk), lambda b, d, j: (d, 0, 0)),
                  pl.BlockSpec((None, hk, C), lambda b, d, j: (d, 0, 0)),
                  pl.BlockSpec((None, hk, RET_DV), lambda b, d, j: (d, 0, 0)),
                  pl.BlockSpec((None, None, hk, RET_DV), lambda b, d, j: (b, d, 0, 0))],
        out_specs=[pl.BlockSpec((C, hv), lambda b, d, j: (out_rows(b, d, j), 0)),
                   pl.BlockSpec((None, None, hk, RET_DV), lambda b, d, j: (b, d, 0, 0))],
        out_shape=[jax.ShapeDtypeStruct((n, hv), MXU),
                   jax.ShapeDtypeStruct((batch, 2, hk, RET_DV), F32)],
        scratch_shapes=[pltpu.VMEM((hk, RET_DV), F32), pltpu.VMEM((nch * C, hv), F32)],
        compiler_params=_params(("parallel", "arbitrary", "arbitrary")),
    )(qb, kb, P, P, dec["dm"], dec["xi"], dec["zetaT"], dec["gc"], r0)


def _decay_tables(logit):
    C = RET_CHUNK
    log_g = jax.nn.log_sigmoid(logit.astype(F32))
    i = jnp.arange(C, dtype=F32)
    diff = i[:, None] - i[None, :]
    lg = log_g[:, :, None, None]
    fwd = jnp.where(diff >= 0, jnp.exp(lg[0] * jnp.maximum(diff, 0.0)), 0.0)
    bwd = jnp.where(diff <= 0, jnp.exp(lg[1] * jnp.maximum(-diff, 0.0)), 0.0)
    rep = lambda a: jnp.repeat(a, RET_DK, axis=0)
    xi_f = jnp.exp(log_g[0][:, None] * (i + 1.0))
    xi_b = jnp.exp(log_g[1][:, None] * (C - i))
    ze_f = jnp.exp(log_g[0][:, None] * (C - 1.0 - i))
    ze_b = jnp.exp(log_g[1][:, None] * i)
    gc = jnp.exp(log_g * C)
    return {
        "dm": jnp.stack([fwd, bwd]),
        "xi": jnp.stack([rep(xi_f).T, rep(xi_b).T]),
        "zetaT": jnp.stack([rep(ze_f), rep(ze_b)]),
        "gc": jnp.broadcast_to(jnp.repeat(gc, RET_DK, axis=1)[:, :, None],
                               (2, RET_HEADS * RET_DK, RET_DV)),
    }


def _merge_kernel(oa_ref, ob_ref, oc_ref, ga_ref, gb_ref, gc_ref, wa_ref, wb_ref, wc_ref, m_ref):
    m = (jax.nn.sigmoid(ga_ref[...]) * jnp.dot(oa_ref[...], wa_ref[...], preferred_element_type=F32)
         + jax.nn.sigmoid(gb_ref[...]) * jnp.dot(ob_ref[...], wb_ref[...], preferred_element_type=F32)
         + jax.nn.sigmoid(gc_ref[...]) * jnp.dot(oc_ref[...], wc_ref[...], preferred_element_type=F32))
    m_ref[...] = m.astype(m_ref.dtype)


def _merge(oa, ob, oc, P, seg, wa, wb, wc, tm):
    n = oa.shape[0]
    D = wa.shape[1]
    tn = _tile(D, 512)
    nb = D // tn

    def rows(a):
        return pl.BlockSpec((tm, a.shape[1]), lambda i, j: (i, 0))

    def gate(name):
        return pl.BlockSpec((tm, tn), lambda i, j, o=seg[name][0] // tn: (i, o + j))

    def wcol(a):
        return pl.BlockSpec((a.shape[0], tn), lambda i, j: (0, j))

    return pl.pallas_call(
        _merge_kernel,
        name="merge",
        grid=(n // tm, nb),
        in_specs=[rows(oa), rows(ob), rows(oc), gate("ga"), gate("gb"), gate("gc"),
                  wcol(wa), wcol(wb), wcol(wc)],
        out_specs=pl.BlockSpec((tm, tn), lambda i, j: (i, j)),
        out_shape=jax.ShapeDtypeStruct((n, D), MXU),
        compiler_params=_params(("parallel", "arbitrary")),
    )(oa, ob, oc, P, P, P, wa, wb, wc)


def _layer_norm(y, g, b):
    mu = jnp.mean(y, axis=-1, keepdims=True)
    yc = y - mu
    return yc * lax.rsqrt(jnp.mean(yc * yc, axis=-1, keepdims=True) + NORM_EPS) * g + b


def _outln_kernel(m_ref, w_ref, x_ref, g1_ref, lg_ref, lb_ref, o_ref, *, alpha):
    o = jnp.dot(m_ref[...], w_ref[...], preferred_element_type=F32)
    o_ref[...] = _layer_norm(alpha * x_ref[...] + g1_ref[...] * o, lg_ref[...], lb_ref[...])


def _outln(m, w, xs, mod3, lg, lb, row_of_tile, alpha, tm):
    n, D = xs.shape
    vec = pl.BlockSpec((1, D), lambda i: (0, 0))
    return pl.pallas_call(
        functools.partial(_outln_kernel, alpha=alpha),
        name="outln",
        grid=(n // tm,),
        in_specs=[pl.BlockSpec((tm, D), lambda i: (i, 0)),
                  pl.BlockSpec((D, D), lambda i: (0, 0)),
                  pl.BlockSpec((tm, D), lambda i: (i, 0)),
                  pl.BlockSpec((None, 1, D), lambda i: (row_of_tile(i, tm), 0, 2)),
                  vec, vec],
        out_specs=pl.BlockSpec((tm, D), lambda i: (i, 0)),
        out_shape=jax.ShapeDtypeStruct((n, D), F32),
        compiler_params=_params(("parallel",)),
    )(m, w, xs, mod3, lg, lb)


def _bitonic_merge(xs):
    n = len(xs)
    if n == 1:
        return xs
    half = n // 2
    hi = [jnp.maximum(xs[i], xs[i + half]) for i in range(half)]
    lo = [jnp.minimum(xs[i], xs[i + half]) for i in range(half)]
    return _bitonic_merge(hi) + _bitonic_merge(lo)


def _sort_desc(xs):
    n = len(xs)
    if n == 1:
        return xs
    return _bitonic_merge(_sort_desc(xs[:n // 2]) + _sort_desc(xs[n // 2:])[::-1])


def _top16_of(groups):
    while len(groups) > 1:
        nxt = []
        for a, b in zip(groups[0::2], groups[1::2]):
            nxt.append(_bitonic_merge([jnp.maximum(a[i], b[PEER_TOPK - 1 - i]) for i in range(PEER_TOPK)]))
        groups = nxt
    return groups[0]


def _top16(xs):
    return _top16_of([_sort_desc(xs[g:g + PEER_TOPK]) for g in range(0, len(xs), PEER_TOPK)])


def _router_kernel(x_ref, sh_ref, sc_ref, wq_ref, k1_ref, k2_ref, k2h_ref,
                   ht_ref, thr_ref, e1_ref, s2_ref, e2_ref, s1_scr, s2_scr):
    K, H = PEER_N_KEYS, PEER_HEADS
    tm = x_ref.shape[0]
    h = x_ref[...] * (1.0 + sc_ref[...]) + sh_ref[...]
    hT = h.T.astype(MXU)
    ht_ref[...] = hT
    qT = jnp.dot(wq_ref[...], hT, preferred_element_type=F32).astype(MXU)
    s1_scr[...] = jnp.dot(k1_ref[...], qT, preferred_element_type=F32).reshape(K, H, tm)
    s2_scr[...] = jnp.dot(k2_ref[...], qT, preferred_element_type=F32).reshape(K, H, tm)
    s2_ref[...] = jnp.dot(k2h_ref[...], qT, preferred_element_type=F32).reshape(H, K, tm)

    def lane_block(c, _):
        ls = pl.ds(pl.multiple_of(c * LANES, LANES), LANES)
        s1 = s1_scr[:, :, ls]
        v1 = _top16([s1[k] for k in range(K)])
        v2 = _top16([s2_scr[k, :, ls] for k in range(K)])
        pairs = [(a, b) for a in range(PEER_TOPK) for b in range(PEER_TOPK)
                 if (a + 1) * (b + 1) <= PEER_TOPK]
        cand = [v1[a] + v2[b] for a, b in pairs]
        pad = [jnp.full_like(cand[0], -jnp.inf)] * (-len(cand) % PEER_TOPK)
        t = _top16(cand + pad)[-1]
        ex1 = [jnp.exp(v - v1[0]) for v in v1]
        ex2 = [jnp.exp(v - v2[0]) for v in v2]
        z = jnp.zeros_like(t)
        for (a, b), cv in zip(pairs, cand):
            z = z + jnp.where(cv >= t, ex1[a] * ex2[b], 0.0)
        inv_z = 1.0 / z
        thr = jnp.full(s1.shape, jnp.inf, F32)
        for b in range(PEER_TOPK):
            thr = jnp.where(s1 + v2[b][None] >= t[None], v2[b][None], thr)
        thr_ref[:, :, ls] = thr
        e1_ref[:, :, ls] = jnp.where(s1 >= v1[-1][None], jnp.exp(s1 - v1[0][None]), 0.0)
        for hh in range(H):
            s2h = s2_ref[hh, :, ls]
            e2_ref[hh, :, ls] = jnp.where(
                s2h >= v2[-1][hh:hh + 1], jnp.exp(s2h - v2[0][hh:hh + 1]) * inv_z[hh:hh + 1], 0.0)
        return 0

    lax.fori_loop(0, tm // LANES, lane_block, 0)


def _router(x1, mod3, wts, row_of_tile, tm):
    n, D = x1.shape
    K, H = PEER_N_KEYS, PEER_HEADS
    HQ = H * PEER_DQ

    def const(a):
        return pl.BlockSpec(a.shape, lambda i: (0,) * a.ndim)

    kh = pl.BlockSpec((K, H, tm), lambda i: (0, 0, i))
    hk = pl.BlockSpec((H, K, tm), lambda i: (0, 0, i))
    return pl.pallas_call(
        _router_kernel,
        name="router",
        grid=(n // tm,),
        in_specs=[pl.BlockSpec((tm, D), lambda i: (i, 0)),
                  pl.BlockSpec((None, 1, D), lambda i: (row_of_tile(i, tm), 0, 3)),
                  pl.BlockSpec((None, 1, D), lambda i: (row_of_tile(i, tm), 0, 4)),
                  const(wts["wqT"]), const(wts["k1kh"]), const(wts["k2kh"]), const(wts["k2hk"])],
        out_specs=[pl.BlockSpec((D, tm), lambda i: (0, i)), kh, kh, hk, hk],
        out_shape=[jax.ShapeDtypeStruct((D, n), MXU),
                   jax.ShapeDtypeStruct((K, H, n), F32), jax.ShapeDtypeStruct((K, H, n), F32),
                   jax.ShapeDtypeStruct((H, K, n), F32), jax.ShapeDtypeStruct((H, K, n), F32)],
        scratch_shapes=[pltpu.VMEM((K, H, tm), F32), pltpu.VMEM((K, H, tm), F32)],
        compiler_params=_params(("parallel",)),
    )(x1, mod3, mod3, wts["wqT"], wts["k1kh"], wts["k2kh"], wts["k2hk"])


PEER_ROWS = 32
PEER_TE = 512


def _peer_kernel(go_ref, ua_ref, ub_ref, u0_ref, vta_ref, vtb_ref, vtl_ref, ht_ref,
                 thr_ref, e1_ref, s2_ref, e2_ref, x_ref, g2_ref, lg_ref, lb_ref, o_ref,
                 acc, sc0, sc1, a0, a1, *, alpha):
    e = pl.program_id(1)
    last = pl.num_programs(1) - 1
    te, tm = sc0.shape
    K = PEER_N_KEYS
    nsub = te // K
    go1, go2 = go_ref[0] != 0, go_ref[1] != 0

    dr = acc.shape[0] // nsub

    def scores(u_tile_ref, sc):
        def piece(sub):
            rows = slice(sub * K, (sub + 1) * K)
            sc[rows, :] = jnp.dot(u_tile_ref[rows, :], ht_ref[...], preferred_element_type=F32)
        return [functools.partial(piece, sub) for sub in range(nsub)]

    def values(vt_tile_ref, a):
        def piece(r):
            rows = slice(r * dr, (r + 1) * dr)
            acc[rows, :] += jnp.dot(vt_tile_ref[rows, :], a[...], preferred_element_type=F32)
        return [functools.partial(piece, r) for r in range(nsub)]

    def gates(sc, a, tile):
        def piece(c, rb):
            ls = slice(c * LANES, (c + 1) * LANES)
            rs = slice(rb * PEER_ROWS, (rb + 1) * PEER_ROWS)
            thr = [thr_ref[tile * nsub + sub, :, ls] for sub in range(nsub)]
            e1r = [e1_ref[tile * nsub + sub, :, ls] for sub in range(nsub)]
            g = [None] * nsub
            for h in range(PEER_HEADS):
                s2p, e2p = s2_ref[h, rs, ls], e2_ref[h, rs, ls]
                for sub in range(nsub):
                    w = jnp.where(s2p >= thr[sub][h:h + 1], e2p, 0.0) * e1r[sub][h:h + 1]
                    g[sub] = w if g[sub] is None else g[sub] + w
            for sub in range(nsub):
                xs = slice(sub * K + rb * PEER_ROWS, sub * K + (rb + 1) * PEER_ROWS)
                x = sc[xs, ls]
                act = 0.5 * x * (1.0 + lax.erf(x * SQRT_HALF))
                a[xs, ls] = (act * g[sub]).astype(a.dtype)
        return [functools.partial(piece, c, rb) for c in range(tm // LANES) for rb in range(K // PEER_ROWS)]

    def interleave(matmul_pieces, vector_pieces):
        per = -(-len(vector_pieces) // len(matmul_pieces))
        for i, mm in enumerate(matmul_pieces):
            mm()
            for vp in vector_pieces[i * per:(i + 1) * per]:
                vp()

    def mix(xs, ys):
        return [p for pair in zip(xs, ys) for p in pair]

    @pl.when(e == 0)
    def _():
        acc[...] = jnp.zeros_like(acc)
        a1[...] = jnp.zeros_like(a1)
        for p in scores(u0_ref, sc0):
            p()

    @pl.when(go1)
    def _():
        interleave(mix(scores(ub_ref, sc1), values(vtb_ref, a1)), gates(sc0, a0, 2 * e))

    @pl.when(go2)
    def _():
        interleave(mix(scores(ua_ref, sc0), values(vta_ref, a0)), gates(sc1, a1, 2 * e + 1))

    @pl.when(e == last)
    def _():
        for p in values(vtl_ref, a1):
            p()
        y = alpha * x_ref[...] + g2_ref[...] * acc[...].T
        o_ref[...] = _layer_norm(y, lg_ref[...], lb_ref[...])


def _peer(x1, routed, u, vt, mod3, lg, lb, row_of_tile, alpha, tm):
    n, D = x1.shape
    K, H = PEER_N_KEYS, PEER_HEADS
    te = PEER_TE
    nt = u.shape[0] // te
    ht, thr, e1, s2, e2 = routed
    once = dict(pipeline_mode=pl.Buffered(1))
    kh = pl.BlockSpec((K, H, tm), lambda i, e: (0, 0, i), **once)
    hk = pl.BlockSpec((H, K, tm), lambda i, e: (0, 0, i), **once)
    vec = pl.BlockSpec((1, D), lambda i, e: (0, 0))
    return pl.pallas_call(
        functools.partial(_peer_kernel, alpha=alpha),
        name="peer",
        grid=(n // tm, nt // 2),
        in_specs=[pl.BlockSpec(memory_space=pltpu.SMEM),
                  pl.BlockSpec((te, D), lambda i, e: (jnp.minimum(2 * e + 2, nt - 1), 0)),
                  pl.BlockSpec((te, D), lambda i, e: (2 * e + 1, 0)),
                  pl.BlockSpec((te, D), lambda i, e: (0, 0), **once),
                  pl.BlockSpec((D, te), lambda i, e: (0, 2 * e)),
                  pl.BlockSpec((D, te), lambda i, e: (0, jnp.maximum(2 * e - 1, 0))),
                  pl.BlockSpec((D, te), lambda i, e: (0, nt - 1), **once),
                  pl.BlockSpec((D, tm), lambda i, e: (0, i), **once),
                  kh, kh, hk, hk,
                  pl.BlockSpec((tm, D), lambda i, e: (i, 0), **once),
                  pl.BlockSpec((None, 1, D), lambda i, e: (row_of_tile(i, tm), 0, 5)),
                  vec, vec],
        out_specs=pl.BlockSpec((tm, D), lambda i, e: (i, 0)),
        out_shape=jax.ShapeDtypeStruct((n, D), F32),
        scratch_shapes=[pltpu.VMEM((D, tm), F32), pltpu.VMEM((te, tm), F32), pltpu.VMEM((te, tm), F32),
                        pltpu.VMEM((te, tm), MXU), pltpu.VMEM((te, tm), MXU)],
        compiler_params=_params(("arbitrary", "arbitrary"), 58),
    )(jnp.ones((2,), jnp.int32), u, u, u, vt, vt, vt, ht, thr, e1, s2, e2, x1, mod3, lg, lb)


def _pad_heads(w, heads, width, lo=0):
    lead = w.shape[:-1]
    w = w.reshape(*lead, heads, width)
    w = jnp.pad(w, [(0, 0)] * len(lead) + [(0, 0), (lo, LANES - lo - width)])
    return w.reshape(*lead, heads * LANES)


def _layer_weights(l, seg, total, w_in, mla_q_norm, mla_w_qup, mla_kv_norm, mla_w_kvup, gqa_q_norm,
                   gqa_k_norm, w_br_a, w_br_b, w_br_c, w_out, peer_w_q, peer_k1, peer_k2, peer_u, peer_v):
    D = w_in.shape[1]
    widths = (MLA_Q_LORA, MLA_KV_LORA, MLA_ROPE, RET_HEADS * RET_DK, RET_HEADS * RET_DK,
              RET_HEADS * RET_DV, RET_HEADS * RET_DV, GQA_HEADS * GQA_DH, GQA_KV_HEADS * GQA_DH,
              GQA_KV_HEADS * GQA_DH, D, D, D)
    names = ("cq", "ckv", "kr", "rq", "rk", "rv", "rg", "gq", "gk", "gv", "ga", "gb", "gc")
    cols = dict(zip(names, jnp.split(w_in[l], np.cumsum(widths)[:-1].tolist(), axis=1)))
    lo = MLA_NOPE
    cols["kr"] = jnp.pad(cols["kr"], ((0, 0), (lo, LANES - lo - MLA_ROPE)))
    order = sorted(seg, key=lambda name: seg[name][0])
    used = seg[order[-1]][0] + seg[order[-1]][1]
    win = jnp.concatenate([cols[name] for name in order] + [jnp.zeros((D, total - used), F32)],
                          axis=1).astype(MXU)
    kv = mla_w_kvup[l].reshape(MLA_KV_LORA, MLA_HEADS, MLA_NOPE + MLA_V)
    eye = jnp.eye(PEER_HEADS, dtype=F32)
    half = PEER_DQ // 2
    k1p = jnp.pad(peer_k1[l], ((0, 0), (0, half)))
    k2p = jnp.pad(peer_k2[l], ((0, 0), (half, 0)))
    HQ = PEER_HEADS * PEER_DQ
    return {
        "win": win,
        "qn": mla_q_norm[l][None], "kvn": mla_kv_norm[l][None],
        "wq": _pad_heads(mla_w_qup[l], MLA_HEADS, MLA_NOPE + MLA_ROPE).astype(MXU),
        "wk": _pad_heads(kv[..., :MLA_NOPE].reshape(MLA_KV_LORA, -1), MLA_HEADS, MLA_NOPE).astype(MXU),
        "wv": _pad_heads(kv[..., MLA_NOPE:].reshape(MLA_KV_LORA, -1), MLA_HEADS, MLA_V).astype(MXU),
        "gqn": gqa_q_norm[l][None], "gkn": gqa_k_norm[l][None],
        "wa": _pad_heads(w_br_a[l].T, MLA_HEADS, MLA_V).T.astype(MXU),
        "wb": w_br_b[l].astype(MXU), "wc": w_br_c[l].astype(MXU), "wo": w_out[l].astype(MXU),
        "wqT": peer_w_q[l].T.astype(MXU),
        "k1kh": jnp.einsum("kd,hg->khgd", k1p, eye).reshape(HQ, HQ).astype(MXU),
        "k2kh": jnp.einsum("kd,hg->khgd", k2p, eye).reshape(HQ, HQ).astype(MXU),
        "k2hk": jnp.einsum("kd,hg->hkgd", k2p, eye).reshape(HQ, HQ).astype(MXU),
        "u": peer_u[l].astype(MXU), "vt": peer_v[l].T.astype(MXU),
    }


def _rope_tables(seq):
    t = jnp.arange(seq, dtype=jnp.int32)
    row, col = (t // GRID_W).astype(F32), (t % GRID_W).astype(F32)

    def tab(r):
        nf = r // 4
        inv = ROPE_BASE ** (-jnp.arange(nf, dtype=F32) / nf)
        ar, ac = row[:, None] * inv[None], col[:, None] * inv[None]
        cos = jnp.concatenate([jnp.cos(ar)] * 2 + [jnp.cos(ac)] * 2, axis=1)
        sin = jnp.concatenate([-jnp.sin(ar), jnp.sin(ar), -jnp.sin(ac), jnp.sin(ac)], axis=1)
        return cos, sin

    ca, sa = tab(MLA_ROPE)
    lo, hi = MLA_NOPE, LANES - MLA_NOPE - MLA_ROPE
    ca = jnp.pad(ca, ((0, 0), (lo, hi)), constant_values=1.0)
    sa = jnp.pad(sa, ((0, 0), (lo, hi)))
    cb, sb = tab(RET_DK)
    cb, sb = jnp.tile(cb, (1, RET_HEADS)), jnp.tile(sb, (1, RET_HEADS))
    cc, sc = tab(GQA_DH)
    return [ca, sa, cb, sb, cc, sc]


def kernel(x, c, ctx, c_ctx, w_mod, b_mod, w_in, mla_q_norm, mla_w_qup, mla_kv_norm, mla_w_kvup, ret_decay_logit, gqa_q_norm, gqa_k_norm, w_br_a, w_br_b, w_br_c, w_out, ln1_g, ln1_b, peer_w_q, peer_k1, peer_k2, peer_u, peer_v, ln2_g, ln2_b):
    B, S, D = x.shape
    CTX = ctx.shape[1]
    L = w_in.shape[0]
    alpha = float((2.0 * L) ** 0.25)
    seg, total = _layout(D)

    tm_l = _tile(S, 512)
    tm_c = _tile(B * CTX, 512)
    tpb = S // tm_l
    lat_row = lambda i, tm: (i * tm) // S
    ctx_row = lambda i, tm: B
    lat_tab = lambda i: i % tpb
    ctx_tab = lambda i: 0

    nrow = -(-(B + 1) // 8) * 8
    cvec = jnp.concatenate([c, c_ctx[None], jnp.zeros((nrow - B - 1, D), F32)], axis=0)
    mod = _modulation(cvec, w_mod, b_mod)

    tabs_l = _rope_tables(S)
    tabs_c = [jnp.ones((tm_c, a.shape[1]), F32) if k % 2 == 0 else jnp.zeros((tm_c, a.shape[1]), F32)
              for k, a in enumerate(tabs_l)]

    xl = x.reshape(B * S, D)
    xc = ctx.reshape(B * CTX, D)
    for l in range(L):
        last = l == L - 1
        W = _layer_weights(l, seg, total, w_in, mla_q_norm, mla_w_qup, mla_kv_norm, mla_w_kvup,
                           gqa_q_norm, gqa_k_norm, w_br_a, w_br_b, w_br_c, w_out,
                           peer_w_q, peer_k1, peer_k2, peer_u, peer_v)
        dec = _decay_tables(ret_decay_logit[l])
        mod3 = mod[l].reshape(nrow, 1, 6 * D)
        lg1, lb1, lg2, lb2 = ln1_g[l][None], ln1_b[l][None], ln2_g[l][None], ln2_b[l][None]

        Pl = _inproj(xl, mod3, W["win"], lat_row, _tile(S, 1024))
        Pc = _inproj(xc, mod3, W["win"], ctx_row, tm_c)
        qa, ka, va, qb, kb, qc, kc, vc = _prep(Pl, seg, W, tabs_l, lat_tab, tm_l)
        qa_c, ka_c, va_c, qb_c, kb_c, qc_c, kc_c, vc_c = _prep(Pc, seg, W, tabs_c, ctx_tab, tm_c)

        oa = _attention(qa, ka_c, va_c, ka, va, B, 1, S)
        oc = _attention(qc, kc_c, vc_c, kc, vc, B, GQA_GROUP, S)
        r0 = jnp.zeros((B, 2, RET_HEADS * RET_DK, RET_DV), F32)
        ob_c, r_ctx = _retention(qb_c, kb_c, Pc, seg, dec, r0, B)
        ob, _ = _retention(qb, kb, Pl, seg, dec, r_ctx, B)

        def tail(xs, oa, ob, oc, P, row_of_tile, tm):
            m = _merge(oa, ob, oc, P, seg, W["wa"], W["wb"], W["wc"], tm)
            x1 = _outln(m, W["wo"], xs, mod3, lg1, lb1, row_of_tile, alpha, _tile(tm, 256))
            routed = _router(x1, mod3, W, row_of_tile, tm)
            return _peer(x1, routed, W["u"], W["vt"], mod3, lg2, lb2, row_of_tile, alpha, tm)

        if not last:
            oa_c = _attention(qa_c, ka_c, va_c, None, None, B, 1, CTX)
            oc_c = _attention(qc_c, kc_c, vc_c, None, None, B, GQA_GROUP, CTX)
            xc = tail(xc, oa_c, ob_c, oc_c, Pc, ctx_row, tm_c)
        xl = tail(xl, oa, ob, oc, Pl, lat_row, tm_l)
    return xl.reshape(B, S, D)
```

```python
import functools

import numpy as np
import jax
import jax.numpy as jnp
from jax import lax
from jax.experimental import pallas as pl
from jax.experimental.pallas import tpu as pltpu

GRID_W = 64
ROPE_BASE = 10000.0
NORM_EPS = 1e-5
RMS_EPS = 1e-6
MLA_HEADS, MLA_Q_LORA, MLA_KV_LORA, MLA_NOPE, MLA_ROPE, MLA_V = 8, 512, 256, 64, 32, 64
MLA_SCALE = (MLA_NOPE + MLA_ROPE) ** -0.5
RET_HEADS, RET_DK, RET_DV, RET_CHUNK = 8, 32, 64, 128
RET_K_SCALE = RET_DK ** -0.5
GQA_HEADS, GQA_KV_HEADS, GQA_DH = 8, 2, 128
GQA_GROUP = GQA_HEADS // GQA_KV_HEADS
GQA_SCALE = GQA_DH ** -0.5
PEER_HEADS, PEER_DQ, PEER_N_KEYS, PEER_TOPK = 8, 128, 128, 16
PEER_N_EXPERTS = PEER_N_KEYS * PEER_N_KEYS

LANES = 128
MXU = jnp.bfloat16
F32 = jnp.float32
NEG = -1e30
SQRT_HALF = float(np.sqrt(0.5))
LOG2E = float(np.log2(np.e))

_SEG = (("ga", None), ("gb", None), ("gc", None), ("gq", GQA_HEADS * GQA_DH),
        ("cq", MLA_Q_LORA), ("rv", RET_HEADS * RET_DV), ("rg", RET_HEADS * RET_DV),
        ("ckv", MLA_KV_LORA), ("rq", RET_HEADS * RET_DK), ("rk", RET_HEADS * RET_DK),
        ("gk", GQA_KV_HEADS * GQA_DH), ("gv", GQA_KV_HEADS * GQA_DH), ("kr", LANES))


def _layout(d_model):
    seg, off = {}, 0
    widths = [(name, d_model if w is None else w) for name, w in _SEG]
    for name, w in sorted(widths, key=lambda nw: -nw[1]):
        assert off % w == 0, (name, off, w)
        seg[name] = (off, w)
        off += w
    total = -(-off // 512) * 512
    return seg, total


def _params(sem, vmem_mb=48, flags=None):
    return pltpu.CompilerParams(dimension_semantics=sem, vmem_limit_bytes=vmem_mb << 20, flags=flags)


def _tile(n, pref):
    t = min(n, pref)
    while n % t:
        t //= 2
    return t


def _mod_kernel(c_ref, w_ref, b_ref, o_ref):
    c = c_ref[...]
    a = (c * jax.nn.sigmoid(c)).astype(MXU)
    o_ref[...] = jnp.dot(a, w_ref[...].astype(MXU), preferred_element_type=F32) + b_ref[...]


def _modulation(cvec, w_mod, b_mod):
    L, D, N6 = w_mod.shape
    R = cvec.shape[0]
    tn = _tile(N6, 1024)
    return pl.pallas_call(
        _mod_kernel,
        name="mod",
        grid=(L, N6 // tn),
        in_specs=[pl.BlockSpec((R, D), lambda l, j: (0, 0)),
                  pl.BlockSpec((None, D, tn), lambda l, j: (l, 0, j)),
                  pl.BlockSpec((None, 1, tn), lambda l, j: (l, 0, j))],
        out_specs=pl.BlockSpec((None, R, tn), lambda l, j: (l, 0, j)),
        out_shape=jax.ShapeDtypeStruct((L, R, N6), F32),
        compiler_params=_params(("parallel", "parallel")),
    )(cvec, w_mod, b_mod.reshape(L, 1, N6))


def _inproj_kernel(x_ref, sh_ref, sc_ref, w_ref, o_ref, h_scr):
    @pl.when(pl.program_id(1) == 0)
    def _():
        h_scr[...] = (x_ref[...] * (1.0 + sc_ref[...]) + sh_ref[...]).astype(h_scr.dtype)

    o_ref[...] = jnp.dot(h_scr[...], w_ref[...], preferred_element_type=F32)


def _inproj(xs, mod3, w, row_of_tile, tm):
    n, D = xs.shape
    NP = w.shape[1]
    tn = 512
    return pl.pallas_call(
        _inproj_kernel,
        name="inproj",
        grid=(n // tm, NP // tn),
        in_specs=[pl.BlockSpec((tm, D), lambda i, j: (i, 0)),
                  pl.BlockSpec((None, 1, D), lambda i, j: (row_of_tile(i, tm), 0, 0)),
                  pl.BlockSpec((None, 1, D), lambda i, j: (row_of_tile(i, tm), 0, 1)),
                  pl.BlockSpec((D, tn), lambda i, j: (0, j))],
        out_specs=pl.BlockSpec((tm, tn), lambda i, j: (i, j)),
        out_shape=jax.ShapeDtypeStruct((n, NP), F32),
        scratch_shapes=[pltpu.VMEM((tm, D), MXU)],
        compiler_params=_params(("parallel", "arbitrary")),
    )(xs, mod3, mod3, w)


def _swap_pairs(x, nf):
    lane = lax.broadcasted_iota(jnp.int32, x.shape, 1)
    up = pltpu.roll(x, LANES - nf, 1)
    down = pltpu.roll(x, nf, 1)
    return jnp.where((lane & nf) == 0, up, down)


def _rope(x, cos, sin, nf):
    return x * cos + _swap_pairs(x, nf) * sin


def _rms(x, g):
    return x * lax.rsqrt(jnp.mean(x * x, axis=-1, keepdims=True) + RMS_EPS) * g


def _prep_kernel(cq_ref, ckv_ref, kr_ref, rq_ref, rk_ref, gq_ref, gk_ref, gv_ref,
                 qn_ref, kvn_ref, wq_ref, wk_ref, wv_ref, gqn_ref, gkn_ref,
                 ca_ref, sa_ref, cb_ref, sb_ref, cc_ref, sc_ref,
                 qa_ref, ka_ref, va_ref, qb_ref, kb_ref, qc_ref, kc_ref, vc_ref):
    cqn = _rms(cq_ref[...], qn_ref[...]).astype(MXU)
    q = jnp.dot(cqn, wq_ref[...], preferred_element_type=F32)
    kvn = _rms(ckv_ref[...], kvn_ref[...]).astype(MXU)
    k = jnp.dot(kvn, wk_ref[...], preferred_element_type=F32)
    v = jnp.dot(kvn, wv_ref[...], preferred_element_type=F32)
    for h in range(MLA_HEADS):
        sl = slice(h * LANES, (h + 1) * LANES)
        va_ref[sl, :] = v[:, sl].T.astype(va_ref.dtype)
    ca, sa = ca_ref[...], sa_ref[...]
    kr = _rope(kr_ref[...], ca, sa, MLA_ROPE // 4)
    for h in range(MLA_HEADS):
        sl = slice(h * LANES, (h + 1) * LANES)
        qa_ref[:, sl] = (_rope(q[:, sl], ca, sa, MLA_ROPE // 4) * (MLA_SCALE * LOG2E)).astype(qa_ref.dtype)
        ka_ref[:, sl] = (k[:, sl] + kr).astype(ka_ref.dtype)
    for half in range(RET_HEADS * RET_DK // LANES):
        sl = slice(half * LANES, (half + 1) * LANES)
        cb, sb = cb_ref[:, sl], sb_ref[:, sl]
        qb_ref[:, sl] = _rope(rq_ref[:, sl], cb, sb, RET_DK // 4)
        kb_ref[:, sl] = _rope(rk_ref[:, sl], cb, sb, RET_DK // 4) * RET_K_SCALE
    cc, sc = cc_ref[...], sc_ref[...]
    for h in range(GQA_HEADS):
        sl = slice(h * LANES, (h + 1) * LANES)
        qc_ref[:, sl] = (_rope(_rms(gq_ref[:, sl], gqn_ref[...]), cc, sc, GQA_DH // 4)
                         * (GQA_SCALE * LOG2E)).astype(qc_ref.dtype)
    for h in range(GQA_KV_HEADS):
        sl = slice(h * LANES, (h + 1) * LANES)
        kc_ref[:, sl] = _rope(_rms(gk_ref[:, sl], gkn_ref[...]), cc, sc, GQA_DH // 4).astype(kc_ref.dtype)
    for h in range(GQA_KV_HEADS):
        sl = slice(h * LANES, (h + 1) * LANES)
        vc_ref[sl, :] = gv_ref[:, sl].T.astype(vc_ref.dtype)


def _prep(P, seg, wts, tabs, tab_of_tile, tm):
    n = P.shape[0]

    def pcol(name):
        off, w = seg[name]
        return pl.BlockSpec((tm, w), lambda i, o=off // w: (i, o))

    def const(a):
        return pl.BlockSpec(a.shape, lambda i: (0,) * a.ndim)

    def tab(a):
        return pl.BlockSpec((tm, a.shape[1]), lambda i: (tab_of_tile(i), 0))

    def rows(w):
        return pl.BlockSpec((tm, w), lambda i: (i, 0))

    consts = [wts["qn"], wts["kvn"], wts["wq"], wts["wk"], wts["wv"], wts["gqn"], wts["gkn"]]
    hq, hk = GQA_HEADS * LANES, GQA_KV_HEADS * LANES
    ha = MLA_HEADS * LANES
    hb = RET_HEADS * RET_DK
    outs = [(ha, MXU), (ha, MXU), (ha, MXU), (hb, F32), (hb, F32), (hq, MXU), (hk, MXU), (hk, MXU)]
    transposed = (2, 7)
    return pl.pallas_call(
        _prep_kernel,
        name="prep",
        grid=(n // tm,),
        in_specs=[pcol(s) for s in ("cq", "ckv", "kr", "rq", "rk", "gq", "gk", "gv")]
        + [const(a) for a in consts] + [tab(a) for a in tabs],
        out_specs=[pl.BlockSpec((w, tm), lambda i: (0, i)) if k in transposed else rows(w)
                   for k, (w, _) in enumerate(outs)],
        out_shape=[jax.ShapeDtypeStruct((w, n) if k in transposed else (n, w), dt)
                   for k, (w, dt) in enumerate(outs)],
        compiler_params=_params(("parallel",)),
    )(*([P] * 8), *consts, *tabs)


def _attn_kernel(*refs, group, tk, has_lat):
    if has_lat:
        q_ref, kc_ref, vc_ref, kl_ref, vl_ref, o_ref, acc_scr, s0_scr, s1_scr = refs
    else:
        q_ref, kc_ref, vc_ref, o_ref, acc_scr, s0_scr, s1_scr = refs
    s_scr = (s0_scr, s1_scr)
    tq = q_ref.shape[0]
    if group > 1:
        q = jnp.concatenate([q_ref[:, g * LANES:(g + 1) * LANES] for g in range(group)], axis=0)
    else:
        q = q_ref[...]
    rows = q.shape[0]
    chunks = [(kc_ref, vc_ref, 0, kc_ref.shape[0])]
    if has_lat:
        chunks += [(kl_ref, vl_ref, c * tk, tk) for c in range(kl_ref.shape[0] // tk)]

    NP = 1

    def scores(c, i):
        k_ref, _, off, size = chunks[c]
        r0, r1 = i * (size // NP), (i + 1) * (size // NP)
        s_scr[c % 2][r0:r1, :] = lax.dot_general(k_ref[off + r0:off + r1, :], q, (((1,), (1,)), ((), ())),
                                                 preferred_element_type=F32)

    def softmax_values(c, j, m, l):
        _, vt_ref, off, size = chunks[c]
        cols = slice(j * (rows // NP), (j + 1) * (rows // NP))
        s = s_scr[c % 2][:size, cols]
        m_new = jnp.maximum(m, jnp.max(s, axis=0, keepdims=True))
        a = jnp.exp2(m - m_new)
        p = jnp.exp2(s - m_new)
        acc_scr[:, cols] = a * acc_scr[:, cols] + jnp.dot(vt_ref[:, off:off + size], p.astype(MXU),
                                                          preferred_element_type=F32)
        return m_new, a * l + jnp.sum(p, axis=0, keepdims=True)

    acc_scr[...] = jnp.zeros_like(acc_scr)
    ml = [(jnp.full((1, rows // NP), NEG, F32), jnp.zeros((1, rows // NP), F32)) for _ in range(NP)]
    for i in range(NP):
        scores(0, i)
    for c in range(len(chunks)):
        for j in range(NP):
            if c + 1 < len(chunks):
                scores(c + 1, j)
            ml[j] = softmax_values(c, j, *ml[j])
    l = jnp.concatenate([l for _, l in ml], axis=1)
    o = acc_scr[...] * (1.0 / l)
    for g in range(group):
        o_ref[:, g * LANES:(g + 1) * LANES] = o[:, g * tq:(g + 1) * tq].T.astype(o_ref.dtype)


def _attention(q, kc, vct, kl, vlt, batch, group, rows_q):
    n, hq = q.shape
    hk = hq // (group * LANES)
    ctx_len = kc.shape[0] // batch
    has_lat = kl is not None
    tq = _tile(rows_q, 512 // group)
    nq = rows_q // tq
    in_specs = [pl.BlockSpec((tq, group * LANES), lambda b, h, i: (b * nq + i, h)),
                pl.BlockSpec((ctx_len, LANES), lambda b, h, i: (b, h)),
                pl.BlockSpec((LANES, ctx_len), lambda b, h, i: (h, b))]
    args = [q, kc, vct]
    tk = 512
    if has_lat:
        seq = kl.shape[0] // batch
        tk = _tile(seq, 1024)
        in_specs += [pl.BlockSpec((seq, LANES), lambda b, h, i: (b, h)),
                     pl.BlockSpec((LANES, seq), lambda b, h, i: (h, b))]
        args += [kl, vlt]
    return pl.pallas_call(
        functools.partial(_attn_kernel, group=group, tk=tk, has_lat=has_lat),
        name=f"attn_g{group}_{'lat' if has_lat else 'ctx'}",
        grid=(batch, hk, nq),
        in_specs=in_specs,
        out_specs=pl.BlockSpec((tq, group * LANES), lambda b, h, i: (b * nq + i, h)),
        out_shape=jax.ShapeDtypeStruct((n, hq), MXU),
        scratch_shapes=[pltpu.VMEM((LANES, group * tq), F32)]
        + [pltpu.VMEM((max(tk, ctx_len), group * tq), F32)] * 2,
        compiler_params=_params(("parallel", "parallel", "arbitrary")),
    )(*args)


def _ret_kernel(q_ref, k_ref, v_ref, g_ref, dm_ref, xi_ref, zeta_ref, gc_ref, r0_ref,
                o_ref, rT_ref, r_scr, yf_scr):
    d = pl.program_id(1)
    j = pl.program_id(2)
    nch = pl.num_programs(2)
    C = RET_CHUNK

    @pl.when(j == 0)
    def _():
        r_scr[...] = r0_ref[...]

    q = q_ref[...]
    k = k_ref[...]
    vb = v_ref[...].astype(MXU)
    qb = q.astype(MXU)
    qx = (q * xi_ref[...]).astype(MXU)
    kT = k.T
    kTb = kT.astype(MXU)
    kzT = (kT * zeta_ref[...]).astype(MXU)
    ys = []
    for h in range(RET_HEADS):
        ks = slice(h * RET_DK, (h + 1) * RET_DK)
        vs = slice(h * RET_DV, (h + 1) * RET_DV)
        r_h = r_scr[ks, :]
        s = jnp.dot(qb[:, ks], kTb[ks, :], preferred_element_type=F32) * dm_ref[h]
        y = (jnp.dot(s.astype(MXU), vb[:, vs], preferred_element_type=F32)
             + jnp.dot(qx[:, ks], r_h.astype(MXU), preferred_element_type=F32))
        r_scr[ks, :] = gc_ref[ks, :] * r_h + jnp.dot(kzT[ks, :], vb[:, vs], preferred_element_type=F32)
        ys.append(y)

    chunk = jnp.where(d == 0, j, nch - 1 - j)
    row0 = pl.multiple_of(chunk * C, C)

    @pl.when(d == 0)
    def _():
        for h in range(RET_HEADS):
            yf_scr[pl.ds(row0, C), h * RET_DV:(h + 1) * RET_DV] = ys[h]

    @pl.when(d == 1)
    def _():
        for h in range(RET_HEADS):
            vs = slice(h * RET_DV, (h + 1) * RET_DV)
            y = ys[h] + yf_scr[pl.ds(row0, C), vs]
            mu = jnp.mean(y, axis=-1, keepdims=True)
            yc = y - mu
            yn = yc * lax.rsqrt(jnp.mean(yc * yc, axis=-1, keepdims=True) + NORM_EPS)
            g = g_ref[:, vs]
            o_ref[:, vs] = (g * jax.nn.sigmoid(g) * yn).astype(o_ref.dtype)

    @pl.when(j == nch - 1)
    def _():
        rT_ref[...] = r_scr[...]


def _retention(qb, kb, P, seg, dec, r0, batch):
    n = qb.shape[0]
    C = RET_CHUNK
    nch = n // batch // C
    hk, hv = RET_HEADS * RET_DK, RET_HEADS * RET_DV
    cv, cg = seg["rv"][0] // hv, seg["rg"][0] // hv

    def chunk_of(d, j):
        return jnp.where(d == 0, j, nch - 1 - j)

    def rows(b, d, j):
        return b * nch + chunk_of(d, j)

    def out_rows(b, d, j):
        return b * nch + jnp.where(d == 0, nch - 1, nch - 1 - j)

    return pl.pallas_call(
        _ret_kernel,
        name="retention",
        grid=(batch, 2, nch),
        in_specs=[pl.BlockSpec((C, hk), lambda b, d, j: (rows(b, d, j), 0)),
                  pl.BlockSpec((C, hk), lambda b, d, j: (rows(b, d, j), 0)),
                  pl.BlockSpec((C, hv), lambda b, d, j: (rows(b, d, j), cv)),
                  pl.BlockSpec((C, hv), lambda b, d, j: (out_rows(b, d, j), cg)),
                  pl.BlockSpec((None, RET_HEADS, C, C), lambda b, d, j: (d, 0, 0, 0)),
                  pl.BlockSpec((None, C, hk), lambda b, d, j: (d, 0, 0)),
                  pl.BlockSpec((None, hk, C), lambda b, d, j: (d, 0, 0)),
                  pl.BlockSpec((None, hk, RET_DV), lambda b, d, j: (d, 0, 0)),
                  pl.BlockSpec((None, None, hk, RET_DV), lambda b, d, j: (b, d, 0, 0))],
        out_specs=[pl.BlockSpec((C, hv), lambda b, d, j: (out_rows(b, d, j), 0)),
                   pl.BlockSpec((None, None, hk, RET_DV), lambda b, d, j: (b, d, 0, 0))],
        out_shape=[jax.ShapeDtypeStruct((n, hv), MXU),
                   jax.ShapeDtypeStruct((batch, 2, hk, RET_DV), F32)],
        scratch_shapes=[pltpu.VMEM((hk, RET_DV), F32), pltpu.VMEM((nch * C, hv), F32)],
        compiler_params=_params(("parallel", "arbitrary", "arbitrary")),
    )(qb, kb, P, P, dec["dm"], dec["xi"], dec["zetaT"], dec["gc"], r0)


def _decay_tables(logit):
    C = RET_CHUNK
    log_g = jax.nn.log_sigmoid(logit.astype(F32))
    i = jnp.arange(C, dtype=F32)
    diff = i[:, None] - i[None, :]
    lg = log_g[:, :, None, None]
    fwd = jnp.where(diff >= 0, jnp.exp(lg[0] * jnp.maximum(diff, 0.0)), 0.0)
    bwd = jnp.where(diff <= 0, jnp.exp(lg[1] * jnp.maximum(-diff, 0.0)), 0.0)
    rep = lambda a: jnp.repeat(a, RET_DK, axis=0)
    xi_f = jnp.exp(log_g[0][:, None] * (i + 1.0))
    xi_b = jnp.exp(log_g[1][:, None] * (C - i))
    ze_f = jnp.exp(log_g[0][:, None] * (C - 1.0 - i))
    ze_b = jnp.exp(log_g[1][:, None] * i)
    gc = jnp.exp(log_g * C)
    return {
        "dm": jnp.stack([fwd, bwd]),
        "xi": jnp.stack([rep(xi_f).T, rep(xi_b).T]),
        "zetaT": jnp.stack([rep(ze_f), rep(ze_b)]),
        "gc": jnp.broadcast_to(jnp.repeat(gc, RET_DK, axis=1)[:, :, None],
                               (2, RET_HEADS * RET_DK, RET_DV)),
    }


def _merge_kernel(oa_ref, ob_ref, oc_ref, ga_ref, gb_ref, gc_ref, wa_ref, wb_ref, wc_ref, m_ref):
    m = (jax.nn.sigmoid(ga_ref[...]) * jnp.dot(oa_ref[...], wa_ref[...], preferred_element_type=F32)
         + jax.nn.sigmoid(gb_ref[...]) * jnp.dot(ob_ref[...], wb_ref[...], preferred_element_type=F32)
         + jax.nn.sigmoid(gc_ref[...]) * jnp.dot(oc_ref[...], wc_ref[...], preferred_element_type=F32))
    m_ref[...] = m.astype(m_ref.dtype)


def _merge(oa, ob, oc, P, seg, wa, wb, wc, tm):
    n = oa.shape[0]
    D = wa.shape[1]
    tn = _tile(D, 512)
    nb = D // tn

    def rows(a):
        return pl.BlockSpec((tm, a.shape[1]), lambda i, j: (i, 0))

    def gate(name):
        return pl.BlockSpec((tm, tn), lambda i, j, o=seg[name][0] // tn: (i, o + j))

    def wcol(a):
        return pl.BlockSpec((a.shape[0], tn), lambda i, j: (0, j))

    return pl.pallas_call(
        _merge_kernel,
        name="merge",
        grid=(n // tm, nb),
        in_specs=[rows(oa), rows(ob), rows(oc), gate("ga"), gate("gb"), gate("gc"),
                  wcol(wa), wcol(wb), wcol(wc)],
        out_specs=pl.BlockSpec((tm, tn), lambda i, j: (i, j)),
        out_shape=jax.ShapeDtypeStruct((n, D), MXU),
        compiler_params=_params(("parallel", "arbitrary")),
    )(oa, ob, oc, P, P, P, wa, wb, wc)


def _layer_norm(y, g, b):
    mu = jnp.mean(y, axis=-1, keepdims=True)
    yc = y - mu
    return yc * lax.rsqrt(jnp.mean(yc * yc, axis=-1, keepdims=True) + NORM_EPS) * g + b


def _outln_kernel(m_ref, w_ref, x_ref, g1_ref, lg_ref, lb_ref, o_ref, *, alpha):
    o = jnp.dot(m_ref[...], w_ref[...], preferred_element_type=F32)
    o_ref[...] = _layer_norm(alpha * x_ref[...] + g1_ref[...] * o, lg_ref[...], lb_ref[...])


def _outln(m, w, xs, mod3, lg, lb, row_of_tile, alpha, tm):
    n, D = xs.shape
    vec = pl.BlockSpec((1, D), lambda i: (0, 0))
    return pl.pallas_call(
        functools.partial(_outln_kernel, alpha=alpha),
        name="outln",
        grid=(n // tm,),
        in_specs=[pl.BlockSpec((tm, D), lambda i: (i, 0)),
                  pl.BlockSpec((D, D), lambda i: (0, 0)),
                  pl.BlockSpec((tm, D), lambda i: (i, 0)),
                  pl.BlockSpec((None, 1, D), lambda i: (row_of_tile(i, tm), 0, 2)),
                  vec, vec],
        out_specs=pl.BlockSpec((tm, D), lambda i: (i, 0)),
        out_shape=jax.ShapeDtypeStruct((n, D), F32),
        compiler_params=_params(("parallel",)),
    )(m, w, xs, mod3, lg, lb)


def _bitonic_merge(xs):
    n = len(xs)
    if n == 1:
        return xs
    half = n // 2
    hi = [jnp.maximum(xs[i], xs[i + half]) for i in range(half)]
    lo = [jnp.minimum(xs[i], xs[i + half]) for i in range(half)]
    return _bitonic_merge(hi) + _bitonic_merge(lo)


def _sort_desc(xs):
    n = len(xs)
    if n == 1:
        return xs
    return _bitonic_merge(_sort_desc(xs[:n // 2]) + _sort_desc(xs[n // 2:])[::-1])


def _top16_of(groups):
    while len(groups) > 1:
        nxt = []
        for a, b in zip(groups[0::2], groups[1::2]):
            nxt.append(_bitonic_merge([jnp.maximum(a[i], b[PEER_TOPK - 1 - i]) for i in range(PEER_TOPK)]))
        groups = nxt
    return groups[0]


def _top16(xs):
    return _top16_of([_sort_desc(xs[g:g + PEER_TOPK]) for g in range(0, len(xs), PEER_TOPK)])


def _router_kernel(x_ref, sh_ref, sc_ref, wq_ref, k1_ref, k2_ref, k2h_ref,
                   ht_ref, thr_ref, e1_ref, s2_ref, e2_ref, s1_scr, s2_scr):
    K, H = PEER_N_KEYS, PEER_HEADS
    tm = x_ref.shape[0]
    h = x_ref[...] * (1.0 + sc_ref[...]) + sh_ref[...]
    hT = h.T.astype(MXU)
    ht_ref[...] = hT
    qT = jnp.dot(wq_ref[...], hT, preferred_element_type=F32).astype(MXU)
    s1_scr[...] = jnp.dot(k1_ref[...], qT, preferred_element_type=F32).reshape(K, H, tm)
    s2_scr[...] = jnp.dot(k2_ref[...], qT, preferred_element_type=F32).reshape(K, H, tm)
    s2_ref[...] = jnp.dot(k2h_ref[...], qT, preferred_element_type=F32).reshape(H, K, tm)

    def lane_block(c, _):
        ls = pl.ds(pl.multiple_of(c * LANES, LANES), LANES)
        s1 = s1_scr[:, :, ls]
        v1 = _top16([s1[k] for k in range(K)])
        v2 = _top16([s2_scr[k, :, ls] for k in range(K)])
        pairs = [(a, b) for a in range(PEER_TOPK) for b in range(PEER_TOPK)
                 if (a + 1) * (b + 1) <= PEER_TOPK]
        cand = [v1[a] + v2[b] for a, b in pairs]
        pad = [jnp.full_like(cand[0], -jnp.inf)] * (-len(cand) % PEER_TOPK)
        t = _top16(cand + pad)[-1]
        ex1 = [jnp.exp(v - v1[0]) for v in v1]
        ex2 = [jnp.exp(v - v2[0]) for v in v2]
        z = jnp.zeros_like(t)
        for (a, b), cv in zip(pairs, cand):
            z = z + jnp.where(cv >= t, ex1[a] * ex2[b], 0.0)
        inv_z = 1.0 / z
        thr = jnp.full(s1.shape, jnp.inf, F32)
        for b in range(PEER_TOPK):
            thr = jnp.where(s1 + v2[b][None] >= t[None], v2[b][None], thr)
        thr_ref[:, :, ls] = thr
        e1_ref[:, :, ls] = jnp.where(s1 >= v1[-1][None], jnp.exp(s1 - v1[0][None]), 0.0)
        for hh in range(H):
            s2h = s2_ref[hh, :, ls]
            e2_ref[hh, :, ls] = jnp.where(
                s2h >= v2[-1][hh:hh + 1], jnp.exp(s2h - v2[0][hh:hh + 1]) * inv_z[hh:hh + 1], 0.0)
        return 0

    lax.fori_loop(0, tm // LANES, lane_block, 0)


def _router(x1, mod3, wts, row_of_tile, tm):
    n, D = x1.shape
    K, H = PEER_N_KEYS, PEER_HEADS
    HQ = H * PEER_DQ

    def const(a):
        return pl.BlockSpec(a.shape, lambda i: (0,) * a.ndim)

    kh = pl.BlockSpec((K, H, tm), lambda i: (0, 0, i))
    hk = pl.BlockSpec((H, K, tm), lambda i: (0, 0, i))
    return pl.pallas_call(
        _router_kernel,
        name="router",
        grid=(n // tm,),
        in_specs=[pl.BlockSpec((tm, D), lambda i: (i, 0)),
                  pl.BlockSpec((None, 1, D), lambda i: (row_of_tile(i, tm), 0, 3)),
                  pl.BlockSpec((None, 1, D), lambda i: (row_of_tile(i, tm), 0, 4)),
                  const(wts["wqT"]), const(wts["k1kh"]), const(wts["k2kh"]), const(wts["k2hk"])],
        out_specs=[pl.BlockSpec((D, tm), lambda i: (0, i)), kh, kh, hk, hk],
        out_shape=[jax.ShapeDtypeStruct((D, n), MXU),
                   jax.ShapeDtypeStruct((K, H, n), F32), jax.ShapeDtypeStruct((K, H, n), F32),
                   jax.ShapeDtypeStruct((H, K, n), F32), jax.ShapeDtypeStruct((H, K, n), F32)],
        scratch_shapes=[pltpu.VMEM((K, H, tm), F32), pltpu.VMEM((K, H, tm), F32)],
        compiler_params=_params(("parallel",)),
    )(x1, mod3, mod3, wts["wqT"], wts["k1kh"], wts["k2kh"], wts["k2hk"])


PEER_ROWS = 32
PEER_TE = 512


def _peer_kernel(go_ref, ua_ref, ub_ref, u0_ref, vta_ref, vtb_ref, vtl_ref, ht_ref,
                 thr_ref, e1_ref, s2_ref, e2_ref, x_ref, g2_ref, lg_ref, lb_ref, o_ref,
                 acc, sc0, sc1, a0, a1, *, alpha):
    e = pl.program_id(1)
    last = pl.num_programs(1) - 1
    te, tm = sc0.shape
    K = PEER_N_KEYS
    nsub = te // K
    go1, go2 = go_ref[0] != 0, go_ref[1] != 0

    nval = nsub
    nsc = nsub
    dr = acc.shape[0] // nval

    def scores(u_tile_ref, sc):
        def piece(i):
            rows = slice(i * (te // nsc), (i + 1) * (te // nsc))
            sc[rows, :] = jnp.dot(u_tile_ref[rows, :], ht_ref[...], preferred_element_type=F32)
        return [functools.partial(piece, i) for i in range(nsc)]

    def values(vt_tile_ref, a):
        def piece(r):
            rows = slice(r * dr, (r + 1) * dr)
            acc[rows, :] += jnp.dot(vt_tile_ref[rows, :], a[...], preferred_element_type=F32)
        return [functools.partial(piece, r) for r in range(nval)]

    def gates(sc, a, tile):
        def piece(c, rb):
            ls = slice(c * LANES, (c + 1) * LANES)
            rs = slice(rb * PEER_ROWS, (rb + 1) * PEER_ROWS)
            thr = [thr_ref[tile * nsub + sub, :, ls] for sub in range(nsub)]
            e1r = [e1_ref[tile * nsub + sub, :, ls] for sub in range(nsub)]
            g = [None] * nsub
            for h in range(PEER_HEADS):
                s2p, e2p = s2_ref[h, rs, ls], e2_ref[h, rs, ls]
                for sub in range(nsub):
                    w = jnp.where(s2p >= thr[sub][h:h + 1], e2p, 0.0) * e1r[sub][h:h + 1]
                    g[sub] = w if g[sub] is None else g[sub] + w
            for sub in range(nsub):
                xs = slice(sub * K + rb * PEER_ROWS, sub * K + (rb + 1) * PEER_ROWS)
                x = sc[xs, ls]
                act = 0.5 * x * (1.0 + lax.erf(x * SQRT_HALF))
                a[xs, ls] = (act * g[sub]).astype(a.dtype)
        return [functools.partial(piece, c, rb) for c in range(tm // LANES) for rb in range(K // PEER_ROWS)]

    def interleave(matmul_pieces, vector_pieces):
        per = -(-len(vector_pieces) // len(matmul_pieces))
        for i, mm in enumerate(matmul_pieces):
            mm()
            for vp in vector_pieces[i * per:(i + 1) * per]:
                vp()

    def mix(xs, ys):
        if len(xs) < len(ys):
            xs, ys = ys, xs
        step = len(xs) // len(ys)
        out = []
        for i, x in enumerate(xs):
            out.append(x)
            if i % step == step - 1 and i // step < len(ys):
                out.append(ys[i // step])
        return out

    @pl.when(e == 0)
    def _():
        acc[...] = jnp.zeros_like(acc)
        a1[...] = jnp.zeros_like(a1)
        for p in scores(u0_ref, sc0):
            p()

    @pl.when(go1)
    def _():
        interleave(mix(scores(ub_ref, sc1), values(vtb_ref, a1)), gates(sc0, a0, 2 * e))

    @pl.when(go2)
    def _():
        interleave(mix(scores(ua_ref, sc0), values(vta_ref, a0)), gates(sc1, a1, 2 * e + 1))

    @pl.when(e == last)
    def _():
        for p in values(vtl_ref, a1):
            p()
        y = alpha * x_ref[...] + g2_ref[...] * acc[...].T
        o_ref[...] = _layer_norm(y, lg_ref[...], lb_ref[...])


def _peer(x1, routed, u, vt, mod3, lg, lb, row_of_tile, alpha, tm):
    n, D = x1.shape
    K, H = PEER_N_KEYS, PEER_HEADS
    te = PEER_TE
    nt = u.shape[0] // te
    ht, thr, e1, s2, e2 = routed
    once = dict(pipeline_mode=pl.Buffered(1))
    kh = pl.BlockSpec((K, H, tm), lambda i, e: (0, 0, i), **once)
    hk = pl.BlockSpec((H, K, tm), lambda i, e: (0, 0, i), **once)
    vec = pl.BlockSpec((1, D), lambda i, e: (0, 0))
    return pl.pallas_call(
        functools.partial(_peer_kernel, alpha=alpha),
        name="peer",
        grid=(n // tm, nt // 2),
        in_specs=[pl.BlockSpec(memory_space=pltpu.SMEM),
                  pl.BlockSpec((te, D), lambda i, e: (jnp.minimum(2 * e + 2, nt - 1), 0)),
                  pl.BlockSpec((te, D), lambda i, e: (2 * e + 1, 0)),
                  pl.BlockSpec((te, D), lambda i, e: (0, 0), **once),
                  pl.BlockSpec((D, te), lambda i, e: (0, 2 * e)),
                  pl.BlockSpec((D, te), lambda i, e: (0, jnp.maximum(2 * e - 1, 0))),
                  pl.BlockSpec((D, te), lambda i, e: (0, nt - 1), **once),
                  pl.BlockSpec((D, tm), lambda i, e: (0, i), **once),
                  kh, kh, hk, hk,
                  pl.BlockSpec((tm, D), lambda i, e: (i, 0), **once),
                  pl.BlockSpec((None, 1, D), lambda i, e: (row_of_tile(i, tm), 0, 5)),
                  vec, vec],
        out_specs=pl.BlockSpec((tm, D), lambda i, e: (i, 0)),
        out_shape=jax.ShapeDtypeStruct((n, D), F32),
        scratch_shapes=[pltpu.VMEM((D, tm), F32), pltpu.VMEM((te, tm), F32), pltpu.VMEM((te, tm), F32),
                        pltpu.VMEM((te, tm), MXU), pltpu.VMEM((te, tm), MXU)],
        compiler_params=_params(("arbitrary", "arbitrary"), 58),
    )(jnp.ones((2,), jnp.int32), u, u, u, vt, vt, vt, ht, thr, e1, s2, e2, x1, mod3, lg, lb)


def _pad_heads(w, heads, width, lo=0):
    lead = w.shape[:-1]
    w = w.reshape(*lead, heads, width)
    w = jnp.pad(w, [(0, 0)] * len(lead) + [(0, 0), (lo, LANES - lo - width)])
    return w.reshape(*lead, heads * LANES)


def _layer_weights(l, seg, total, w_in, mla_q_norm, mla_w_qup, mla_kv_norm, mla_w_kvup, gqa_q_norm,
                   gqa_k_norm, w_br_a, w_br_b, w_br_c, w_out, peer_w_q, peer_k1, peer_k2, peer_u, peer_v):
    D = w_in.shape[1]
    widths = (MLA_Q_LORA, MLA_KV_LORA, MLA_ROPE, RET_HEADS * RET_DK, RET_HEADS * RET_DK,
              RET_HEADS * RET_DV, RET_HEADS * RET_DV, GQA_HEADS * GQA_DH, GQA_KV_HEADS * GQA_DH,
              GQA_KV_HEADS * GQA_DH, D, D, D)
    names = ("cq", "ckv", "kr", "rq", "rk", "rv", "rg", "gq", "gk", "gv", "ga", "gb", "gc")
    cols = dict(zip(names, jnp.split(w_in[l], np.cumsum(widths)[:-1].tolist(), axis=1)))
    lo = MLA_NOPE
    cols["kr"] = jnp.pad(cols["kr"], ((0, 0), (lo, LANES - lo - MLA_ROPE)))
    order = sorted(seg, key=lambda name: seg[name][0])
    used = seg[order[-1]][0] + seg[order[-1]][1]
    win = jnp.concatenate([cols[name] for name in order] + [jnp.zeros((D, total - used), F32)],
                          axis=1).astype(MXU)
    kv = mla_w_kvup[l].reshape(MLA_KV_LORA, MLA_HEADS, MLA_NOPE + MLA_V)
    eye = jnp.eye(PEER_HEADS, dtype=F32)
    half = PEER_DQ // 2
    k1p = jnp.pad(peer_k1[l], ((0, 0), (0, half)))
    k2p = jnp.pad(peer_k2[l], ((0, 0), (half, 0)))
    HQ = PEER_HEADS * PEER_DQ
    return {
        "win": win,
        "qn": mla_q_norm[l][None], "kvn": mla_kv_norm[l][None],
        "wq": _pad_heads(mla_w_qup[l], MLA_HEADS, MLA_NOPE + MLA_ROPE).astype(MXU),
        "wk": _pad_heads(kv[..., :MLA_NOPE].reshape(MLA_KV_LORA, -1), MLA_HEADS, MLA_NOPE).astype(MXU),
        "wv": _pad_heads(kv[..., MLA_NOPE:].reshape(MLA_KV_LORA, -1), MLA_HEADS, MLA_V).astype(MXU),
        "gqn": gqa_q_norm[l][None], "gkn": gqa_k_norm[l][None],
        "wa": _pad_heads(w_br_a[l].T, MLA_HEADS, MLA_V).T.astype(MXU),
        "wb": w_br_b[l].astype(MXU), "wc": w_br_c[l].astype(MXU), "wo": w_out[l].astype(MXU),
        "wqT": peer_w_q[l].T.astype(MXU),
        "k1kh": jnp.einsum("kd,hg->khgd", k1p, eye).reshape(HQ, HQ).astype(MXU),
        "k2kh": jnp.einsum("kd,hg->khgd", k2p, eye).reshape(HQ, HQ).astype(MXU),
        "k2hk": jnp.einsum("kd,hg->hkgd", k2p, eye).reshape(HQ, HQ).astype(MXU),
        "u": peer_u[l].astype(MXU), "vt": peer_v[l].T.astype(MXU),
    }


def _rope_tables(seq):
    t = jnp.arange(seq, dtype=jnp.int32)
    row, col = (t // GRID_W).astype(F32), (t % GRID_W).astype(F32)

    def tab(r):
        nf = r // 4
        inv = ROPE_BASE ** (-jnp.arange(nf, dtype=F32) / nf)
        ar, ac = row[:, None] * inv[None], col[:, None] * inv[None]
        cos = jnp.concatenate([jnp.cos(ar)] * 2 + [jnp.cos(ac)] * 2, axis=1)
        sin = jnp.concatenate([-jnp.sin(ar), jnp.sin(ar), -jnp.sin(ac), jnp.sin(ac)], axis=1)
        return cos, sin

    ca, sa = tab(MLA_ROPE)
    lo, hi = MLA_NOPE, LANES - MLA_NOPE - MLA_ROPE
    ca = jnp.pad(ca, ((0, 0), (lo, hi)), constant_values=1.0)
    sa = jnp.pad(sa, ((0, 0), (lo, hi)))
    cb, sb = tab(RET_DK)
    cb, sb = jnp.tile(cb, (1, RET_HEADS)), jnp.tile(sb, (1, RET_HEADS))
    cc, sc = tab(GQA_DH)
    return [ca, sa, cb, sb, cc, sc]


def kernel(x, c, ctx, c_ctx, w_mod, b_mod, w_in, mla_q_norm, mla_w_qup, mla_kv_norm, mla_w_kvup, ret_decay_logit, gqa_q_norm, gqa_k_norm, w_br_a, w_br_b, w_br_c, w_out, ln1_g, ln1_b, peer_w_q, peer_k1, peer_k2, peer_u, peer_v, ln2_g, ln2_b):
    B, S, D = x.shape
    CTX = ctx.shape[1]
    L = w_in.shape[0]
    alpha = float((2.0 * L) ** 0.25)
    seg, total = _layout(D)

    tm_l = _tile(S, 512)
    tm_c = _tile(B * CTX, 512)
    tpb = S // tm_l
    lat_row = lambda i, tm: (i * tm) // S
    ctx_row = lambda i, tm: B
    lat_tab = lambda i: i % tpb
    ctx_tab = lambda i: 0

    nrow = -(-(B + 1) // 8) * 8
    cvec = jnp.concatenate([c, c_ctx[None], jnp.zeros((nrow - B - 1, D), F32)], axis=0)
    mod = _modulation(cvec, w_mod, b_mod)

    tabs_l = _rope_tables(S)
    tabs_c = [jnp.ones((tm_c, a.shape[1]), F32) if k % 2 == 0 else jnp.zeros((tm_c, a.shape[1]), F32)
              for k, a in enumerate(tabs_l)]

    xl = x.reshape(B * S, D)
    xc = ctx.reshape(B * CTX, D)
    for l in range(L):
        last = l == L - 1
        W = _layer_weights(l, seg, total, w_in, mla_q_norm, mla_w_qup, mla_kv_norm, mla_w_kvup,
                           gqa_q_norm, gqa_k_norm, w_br_a, w_br_b, w_br_c, w_out,
                           peer_w_q, peer_k1, peer_k2, peer_u, peer_v)
        dec = _decay_tables(ret_decay_logit[l])
        mod3 = mod[l].reshape(nrow, 1, 6 * D)
        lg1, lb1, lg2, lb2 = ln1_g[l][None], ln1_b[l][None], ln2_g[l][None], ln2_b[l][None]

        Pl = _inproj(xl, mod3, W["win"], lat_row, _tile(S, 1024))
        Pc = _inproj(xc, mod3, W["win"], ctx_row, tm_c)
        qa, ka, va, qb, kb, qc, kc, vc = _prep(Pl, seg, W, tabs_l, lat_tab, tm_l)
        qa_c, ka_c, va_c, qb_c, kb_c, qc_c, kc_c, vc_c = _prep(Pc, seg, W, tabs_c, ctx_tab, tm_c)

        oa = _attention(qa, ka_c, va_c, ka, va, B, 1, S)
        oc = _attention(qc, kc_c, vc_c, kc, vc, B, GQA_GROUP, S)
        r0 = jnp.zeros((B, 2, RET_HEADS * RET_DK, RET_DV), F32)
        ob_c, r_ctx = _retention(qb_c, kb_c, Pc, seg, dec, r0, B)
        ob, _ = _retention(qb, kb, Pl, seg, dec, r_ctx, B)

        def tail(xs, oa, ob, oc, P, row_of_tile, tm):
            m = _merge(oa, ob, oc, P, seg, W["wa"], W["wb"], W["wc"], tm)
            x1 = _outln(m, W["wo"], xs, mod3, lg1, lb1, row_of_tile, alpha, _tile(tm, 256))
            routed = _router(x1, mod3, W, row_of_tile, tm)
            return _peer(x1, routed, W["u"], W["vt"], mod3, lg2, lb2, row_of_tile, alpha, tm)

        if not last:
            oa_c = _attention(qa_c, ka_c, va_c, None, None, B, 1, CTX)
            oc_c = _attention(qc_c, kc_c, vc_c, None, None, B, GQA_GROUP, CTX)
            xc = tail(xc, oa_c, ob_c, oc_c, Pc, ctx_row, tm_c)
        xl = tail(xl, oa, ob, oc, Pl, lat_row, tm_l)
    return xl.reshape(B, S, D)
```

```python
import functools

import numpy as np
import jax
import jax.numpy as jnp
from jax import lax
from jax.experimental import pallas as pl
from jax.experimental.pallas import tpu as pltpu

GRID_W = 64
ROPE_BASE = 10000.0
NORM_EPS = 1e-5
RMS_EPS = 1e-6
MLA_HEADS, MLA_Q_LORA, MLA_KV_LORA, MLA_NOPE, MLA_ROPE, MLA_V = 8, 512, 256, 64, 32, 64
MLA_SCALE = (MLA_NOPE + MLA_ROPE) ** -0.5
RET_HEADS, RET_DK, RET_DV, RET_CHUNK = 8, 32, 64, 128
RET_K_SCALE = RET_DK ** -0.5
GQA_HEADS, GQA_KV_HEADS, GQA_DH = 8, 2, 128
GQA_GROUP = GQA_HEADS // GQA_KV_HEADS
GQA_SCALE = GQA_DH ** -0.5
PEER_HEADS, PEER_DQ, PEER_N_KEYS, PEER_TOPK = 8, 128, 128, 16
PEER_N_EXPERTS = PEER_N_KEYS * PEER_N_KEYS

LANES = 128
MXU = jnp.bfloat16
F32 = jnp.float32
NEG = -1e30
SQRT_HALF = float(np.sqrt(0.5))
LOG2E = float(np.log2(np.e))

_SEG = (("ga", None), ("gb", None), ("gc", None), ("gq", GQA_HEADS * GQA_DH),
        ("cq", MLA_Q_LORA), ("rv", RET_HEADS * RET_DV), ("rg", RET_HEADS * RET_DV),
        ("ckv", MLA_KV_LORA), ("rq", RET_HEADS * RET_DK), ("rk", RET_HEADS * RET_DK),
        ("gk", GQA_KV_HEADS * GQA_DH), ("gv", GQA_KV_HEADS * GQA_DH), ("kr", LANES))


def _layout(d_model):
    seg, off = {}, 0
    widths = [(name, d_model if w is None else w) for name, w in _SEG]
    for name, w in sorted(widths, key=lambda nw: -nw[1]):
        assert off % w == 0, (name, off, w)
        seg[name] = (off, w)
        off += w
    total = -(-off // 512) * 512
    return seg, total


def _params(sem, vmem_mb=48, flags=None):
    return pltpu.CompilerParams(dimension_semantics=sem, vmem_limit_bytes=vmem_mb << 20, flags=flags)


def _tile(n, pref):
    t = min(n, pref)
    while n % t:
        t //= 2
    return t


def _mod_kernel(c_ref, w_ref, b_ref, o_ref):
    c = c_ref[...]
    a = (c * jax.nn.sigmoid(c)).astype(MXU)
    o_ref[...] = jnp.dot(a, w_ref[...].astype(MXU), preferred_element_type=F32) + b_ref[...]


def _modulation(cvec, w_mod, b_mod):
    L, D, N6 = w_mod.shape
    R = cvec.shape[0]
    tn = _tile(N6, 1024)
    return pl.pallas_call(
        _mod_kernel,
        name="mod",
        grid=(L, N6 // tn),
        in_specs=[pl.BlockSpec((R, D), lambda l, j: (0, 0)),
                  pl.BlockSpec((None, D, tn), lambda l, j: (l, 0, j)),
                  pl.BlockSpec((None, 1, tn), lambda l, j: (l, 0, j))],
        out_specs=pl.BlockSpec((None, R, tn), lambda l, j: (l, 0, j)),
        out_shape=jax.ShapeDtypeStruct((L, R, N6), F32),
        compiler_params=_params(("parallel", "parallel")),
    )(cvec, w_mod, b_mod.reshape(L, 1, N6))


def _inproj_kernel(x_ref, sh_ref, sc_ref, w_ref, o_ref, h_scr):
    @pl.when(pl.program_id(1) == 0)
    def _():
        h_scr[...] = (x_ref[...] * (1.0 + sc_ref[...]) + sh_ref[...]).astype(h_scr.dtype)

    o_ref[...] = jnp.dot(h_scr[...], w_ref[...], preferred_element_type=F32)


def _inproj(xs, mod3, w, row_of_tile, tm):
    n, D = xs.shape
    NP = w.shape[1]
    tn = 512
    return pl.pallas_call(
        _inproj_kernel,
        name="inproj",
        grid=(n // tm, NP // tn),
        in_specs=[pl.BlockSpec((tm, D), lambda i, j: (i, 0)),
                  pl.BlockSpec((None, 1, D), lambda i, j: (row_of_tile(i, tm), 0, 0)),
                  pl.BlockSpec((None, 1, D), lambda i, j: (row_of_tile(i, tm), 0, 1)),
                  pl.BlockSpec((D, tn), lambda i, j: (0, j))],
        out_specs=pl.BlockSpec((tm, tn), lambda i, j: (i, j)),
        out_shape=jax.ShapeDtypeStruct((n, NP), F32),
        scratch_shapes=[pltpu.VMEM((tm, D), MXU)],
        compiler_params=_params(("parallel", "arbitrary")),
    )(xs, mod3, mod3, w)


def _swap_pairs(x, nf):
    lane = lax.broadcasted_iota(jnp.int32, x.shape, 1)
    up = pltpu.roll(x, LANES - nf, 1)
    down = pltpu.roll(x, nf, 1)
    return jnp.where((lane & nf) == 0, up, down)


def _rope(x, cos, sin, nf):
    return x * cos + _swap_pairs(x, nf) * sin


def _rms(x, g):
    return x * lax.rsqrt(jnp.mean(x * x, axis=-1, keepdims=True) + RMS_EPS) * g


def _prep_kernel(cq_ref, ckv_ref, kr_ref, rq_ref, rk_ref, gq_ref, gk_ref, gv_ref,
                 qn_ref, kvn_ref, wq_ref, wk_ref, wv_ref, gqn_ref, gkn_ref,
                 ca_ref, sa_ref, cb_ref, sb_ref, cc_ref, sc_ref,
                 qa_ref, ka_ref, va_ref, qb_ref, kb_ref, qc_ref, kc_ref, vc_ref):
    cqn = _rms(cq_ref[...], qn_ref[...]).astype(MXU)
    q = jnp.dot(cqn, wq_ref[...], preferred_element_type=F32)
    kvn = _rms(ckv_ref[...], kvn_ref[...]).astype(MXU)
    k = jnp.dot(kvn, wk_ref[...], preferred_element_type=F32)
    v = jnp.dot(kvn, wv_ref[...], preferred_element_type=F32)
    for h in range(MLA_HEADS):
        sl = slice(h * LANES, (h + 1) * LANES)
        va_ref[sl, :] = v[:, sl].T.astype(va_ref.dtype)
    ca, sa = ca_ref[...], sa_ref[...]
    kr = _rope(kr_ref[...], ca, sa, MLA_ROPE // 4)
    for h in range(MLA_HEADS):
        sl = slice(h * LANES, (h + 1) * LANES)
        qa_ref[:, sl] = (_rope(q[:, sl], ca, sa, MLA_ROPE // 4) * (MLA_SCALE * LOG2E)).astype(qa_ref.dtype)
        ka_ref[:, sl] = (k[:, sl] + kr).astype(ka_ref.dtype)
    for half in range(RET_HEADS * RET_DK // LANES):
        sl = slice(half * LANES, (half + 1) * LANES)
        cb, sb = cb_ref[:, sl], sb_ref[:, sl]
        qb_ref[:, sl] = _rope(rq_ref[:, sl], cb, sb, RET_DK // 4)
        kb_ref[:, sl] = _rope(rk_ref[:, sl], cb, sb, RET_DK // 4) * RET_K_SCALE
    cc, sc = cc_ref[...], sc_ref[...]
    for h in range(GQA_HEADS):
        sl = slice(h * LANES, (h + 1) * LANES)
        qc_ref[:, sl] = (_rope(_rms(gq_ref[:, sl], gqn_ref[...]), cc, sc, GQA_DH // 4)
                         * (GQA_SCALE * LOG2E)).astype(qc_ref.dtype)
    for h in range(GQA_KV_HEADS):
        sl = slice(h * LANES, (h + 1) * LANES)
        kc_ref[:, sl] = _rope(_rms(gk_ref[:, sl], gkn_ref[...]), cc, sc, GQA_DH // 4).astype(kc_ref.dtype)
    for h in range(GQA_KV_HEADS):
        sl = slice(h * LANES, (h + 1) * LANES)
        vc_ref[sl, :] = gv_ref[:, sl].T.astype(vc_ref.dtype)


def _prep(P, seg, wts, tabs, tab_of_tile, tm):
    n = P.shape[0]

    def pcol(name):
        off, w = seg[name]
        return pl.BlockSpec((tm, w), lambda i, o=off // w: (i, o))

    def const(a):
        return pl.BlockSpec(a.shape, lambda i: (0,) * a.ndim)

    def tab(a):
        return pl.BlockSpec((tm, a.shape[1]), lambda i: (tab_of_tile(i), 0))

    def rows(w):
        return pl.BlockSpec((tm, w), lambda i: (i, 0))

    consts = [wts["qn"], wts["kvn"], wts["wq"], wts["wk"], wts["wv"], wts["gqn"], wts["gkn"]]
    hq, hk = GQA_HEADS * LANES, GQA_KV_HEADS * LANES
    ha = MLA_HEADS * LANES
    hb = RET_HEADS * RET_DK
    outs = [(ha, MXU), (ha, MXU), (ha, MXU), (hb, F32), (hb, F32), (hq, MXU), (hk, MXU), (hk, MXU)]
    transposed = (2, 7)
    return pl.pallas_call(
        _prep_kernel,
        name="prep",
        grid=(n // tm,),
        in_specs=[pcol(s) for s in ("cq", "ckv", "kr", "rq", "rk", "gq", "gk", "gv")]
        + [const(a) for a in consts] + [tab(a) for a in tabs],
        out_specs=[pl.BlockSpec((w, tm), lambda i: (0, i)) if k in transposed else rows(w)
                   for k, (w, _) in enumerate(outs)],
        out_shape=[jax.ShapeDtypeStruct((w, n) if k in transposed else (n, w), dt)
                   for k, (w, dt) in enumerate(outs)],
        compiler_params=_params(("parallel",)),
    )(*([P] * 8), *consts, *tabs)


def _attn_kernel(*refs, group, tk, has_lat):
    if has_lat:
        q_ref, kc_ref, vc_ref, kl_ref, vl_ref, o_ref, acc_scr, s0_scr, s1_scr = refs
    else:
        q_ref, kc_ref, vc_ref, o_ref, acc_scr, s0_scr, s1_scr = refs
    s_scr = (s0_scr, s1_scr)
    tq = q_ref.shape[0]
    if group > 1:
        q = jnp.concatenate([q_ref[:, g * LANES:(g + 1) * LANES] for g in range(group)], axis=0)
    else:
        q = q_ref[...]
    rows = q.shape[0]
    chunks = [(kc_ref, vc_ref, 0, kc_ref.shape[0])]
    if has_lat:
        chunks += [(kl_ref, vl_ref, c * tk, tk) for c in range(kl_ref.shape[0] // tk)]

    NP = 1

    def scores(c, i):
        k_ref, _, off, size = chunks[c]
        r0, r1 = i * (size // NP), (i + 1) * (size // NP)
        s_scr[c % 2][r0:r1, :] = lax.dot_general(k_ref[off + r0:off + r1, :], q, (((1,), (1,)), ((), ())),
                                                 preferred_element_type=F32)

    def softmax_values(c, j, m, l):
        _, vt_ref, off, size = chunks[c]
        cols = slice(j * (rows // NP), (j + 1) * (rows // NP))
        s = s_scr[c % 2][:size, cols]
        m_new = jnp.maximum(m, jnp.max(s, axis=0, keepdims=True))
        a = jnp.exp2(m - m_new)
        p = jnp.exp2(s - m_new)
        acc_scr[:, cols] = a * acc_scr[:, cols] + jnp.dot(vt_ref[:, off:off + size], p.astype(MXU),
                                                          preferred_element_type=F32)
        return m_new, a * l + jnp.sum(p, axis=0, keepdims=True)

    acc_scr[...] = jnp.zeros_like(acc_scr)
    ml = [(jnp.full((1, rows // NP), NEG, F32), jnp.zeros((1, rows // NP), F32)) for _ in range(NP)]
    for i in range(NP):
        scores(0, i)
    for c in range(len(chunks)):
        for j in range(NP):
            if c + 1 < len(chunks):
                scores(c + 1, j)
            ml[j] = softmax_values(c, j, *ml[j])
    l = jnp.concatenate([l for _, l in ml], axis=1)
    o = acc_scr[...] * (1.0 / l)
    for g in range(group):
        o_ref[:, g * LANES:(g + 1) * LANES] = o[:, g * tq:(g + 1) * tq].T.astype(o_ref.dtype)


def _attention(q, kc, vct, kl, vlt, batch, group, rows_q):
    n, hq = q.shape
    hk = hq // (group * LANES)
    ctx_len = kc.shape[0] // batch
    has_lat = kl is not None
    tq = _tile(rows_q, 512 // group)
    nq = rows_q // tq
    in_specs = [pl.BlockSpec((tq, group * LANES), lambda b, h, i: (b * nq + i, h)),
                pl.BlockSpec((ctx_len, LANES), lambda b, h, i: (b, h)),
                pl.BlockSpec((LANES, ctx_len), lambda b, h, i: (h, b))]
    args = [q, kc, vct]
    tk = 512
    if has_lat:
        seq = kl.shape[0] // batch
        tk = _tile(seq, 1024)
        in_specs += [pl.BlockSpec((seq, LANES), lambda b, h, i: (b, h)),
                     pl.BlockSpec((LANES, seq), lambda b, h, i: (h, b))]
        args += [kl, vlt]
    return pl.pallas_call(
        functools.partial(_attn_kernel, group=group, tk=tk, has_lat=has_lat),
        name=f"attn_g{group}_{'lat' if has_lat else 'ctx'}",
        grid=(batch, hk, nq),
        in_specs=in_specs,
        out_specs=pl.BlockSpec((tq, group * LANES), lambda b, h, i: (b * nq + i, h)),
        out_shape=jax.ShapeDtypeStruct((n, hq), MXU),
        scratch_shapes=[pltpu.VMEM((LANES, group * tq), F32)]
        + [pltpu.VMEM((max(tk, ctx_len), group * tq), F32)] * 2,
        compiler_params=_params(("parallel", "parallel", "arbitrary")),
    )(*args)


def _ret_kernel(qf_ref, kf_ref, vf_ref, gf_ref, qb_ref, kb_ref, vb_ref, gb_ref,
                dm_ref, xi_ref, zeta_ref, gc_ref, avg_ref, r0_ref,
                of_ref, ob_ref, rT_ref, r_scr, yf_scr, yb_scr):
    j = pl.program_id(1)
    nch = pl.num_programs(1)
    C = RET_CHUNK

    @pl.when(j == 0)
    def _():
        r_scr[...] = r0_ref[...]

    def direction(d, q_ref, k_ref, v_ref):
        q = q_ref[...]
        k = k_ref[...]
        vb = v_ref[...].astype(MXU)
        qb = q.astype(MXU)
        qx = (q * xi_ref[d]).astype(MXU)
        kT = k.T
        kTb = kT.astype(MXU)
        kzT = (kT * zeta_ref[d]).astype(MXU)
        ys = []
        for h in range(RET_HEADS):
            ks = slice(h * RET_DK, (h + 1) * RET_DK)
            vs = slice(h * RET_DV, (h + 1) * RET_DV)
            r_h = r_scr[d, ks, :]
            s = jnp.dot(qb[:, ks], kTb[ks, :], preferred_element_type=F32) * dm_ref[d, h]
            y = (jnp.dot(s.astype(MXU), vb[:, vs], preferred_element_type=F32)
                 + jnp.dot(qx[:, ks], r_h.astype(MXU), preferred_element_type=F32))
            r_scr[d, ks, :] = gc_ref[d, ks, :] * r_h + jnp.dot(kzT[ks, :], vb[:, vs],
                                                               preferred_element_type=F32)
            ys.append(y)
        return ys

    ys_f = direction(0, qf_ref, kf_ref, vf_ref)
    ys_b = direction(1, qb_ref, kb_ref, vb_ref)
    row_f = pl.multiple_of(j * C, C)
    row_b = pl.multiple_of((nch - 1 - j) * C, C)

    y_f = jnp.concatenate(ys_f, axis=1)
    y_b = jnp.concatenate(ys_b, axis=1)

    @pl.when(2 * j < nch - 1)
    def _():
        yf_scr[pl.ds(row_f, C), :] = y_f
        yb_scr[pl.ds(row_b, C), :] = y_b

    def group_mean(a):
        hi = a.astype(MXU)
        lo = (a - hi.astype(F32)).astype(MXU)
        return (jnp.dot(hi, avg_ref[...], preferred_element_type=F32)
                + jnp.dot(lo, avg_ref[...], preferred_element_type=F32))

    def finish(y_here, other, row, g_ref, o_ref):
        y = y_here + other[pl.ds(row, C), :]
        yc = y - group_mean(y)
        yn = yc * lax.rsqrt(group_mean(yc * yc) + NORM_EPS)
        g = g_ref[...]
        o_ref[...] = (g * jax.nn.sigmoid(g) * yn).astype(o_ref.dtype)

    @pl.when(2 * j >= nch)
    def _():
        finish(y_f, yb_scr, row_f, gf_ref, of_ref)
        finish(y_b, yf_scr, row_b, gb_ref, ob_ref)

    @pl.when(j == nch - 1)
    def _():
        rT_ref[...] = r_scr[...]


def _retention(qb, kb, P, seg, dec, r0, batch):
    n = qb.shape[0]
    C = RET_CHUNK
    nch = n // batch // C
    assert nch % 2 == 0
    hk, hv = RET_HEADS * RET_DK, RET_HEADS * RET_DV
    cv, cg = seg["rv"][0] // hv, seg["rg"][0] // hv
    half = nch // 2

    fwd = lambda b, j: b * nch + j
    bwd = lambda b, j: b * nch + nch - 1 - j
    fwd_out = lambda b, j: b * nch + jnp.maximum(j, half)
    bwd_out = lambda b, j: b * nch + nch - 1 - jnp.maximum(j, half)
    whole = lambda a: pl.BlockSpec(a.shape, lambda b, j: (0,) * a.ndim)
    head_of = np.arange(hv) // RET_DV
    avg = jnp.asarray((head_of[:, None] == head_of[None, :]).astype(np.float32) / RET_DV, MXU)
    tabs = [dec["dm"], dec["xi"], dec["zetaT"], dec["gc"], avg]
    o_f, o_b, r_t = pl.pallas_call(
        _ret_kernel,
        name="retention",
        grid=(batch, nch),
        in_specs=[pl.BlockSpec((C, hk), lambda b, j: (fwd(b, j), 0)),
                  pl.BlockSpec((C, hk), lambda b, j: (fwd(b, j), 0)),
                  pl.BlockSpec((C, hv), lambda b, j: (fwd(b, j), cv)),
                  pl.BlockSpec((C, hv), lambda b, j: (fwd_out(b, j), cg)),
                  pl.BlockSpec((C, hk), lambda b, j: (bwd(b, j), 0)),
                  pl.BlockSpec((C, hk), lambda b, j: (bwd(b, j), 0)),
                  pl.BlockSpec((C, hv), lambda b, j: (bwd(b, j), cv)),
                  pl.BlockSpec((C, hv), lambda b, j: (bwd_out(b, j), cg))]
        + [whole(t) for t in tabs]
        + [pl.BlockSpec((None, 2, hk, RET_DV), lambda b, j: (b, 0, 0, 0))],
        out_specs=[pl.BlockSpec((C, hv), lambda b, j: (fwd_out(b, j), 0)),
                   pl.BlockSpec((C, hv), lambda b, j: (bwd_out(b, j), 0)),
                   pl.BlockSpec((None, 2, hk, RET_DV), lambda b, j: (b, 0, 0, 0))],
        out_shape=[jax.ShapeDtypeStruct((n, hv), MXU), jax.ShapeDtypeStruct((n, hv), MXU),
                   jax.ShapeDtypeStruct((batch, 2, hk, RET_DV), F32)],
        scratch_shapes=[pltpu.VMEM((2, hk, RET_DV), F32), pltpu.VMEM((nch * C, hv), F32),
                        pltpu.VMEM((nch * C, hv), F32)],
        compiler_params=_params(("parallel", "arbitrary")),
    )(qb, kb, P, P, qb, kb, P, P, *tabs, r0)
    o_f = o_f.reshape(batch, 2, half * C, hv)
    o_b = o_b.reshape(batch, 2, half * C, hv)
    return jnp.stack([o_b[:, 0], o_f[:, 1]], axis=1).reshape(n, hv), r_t


def _decay_tables(logit):
    C = RET_CHUNK
    log_g = jax.nn.log_sigmoid(logit.astype(F32))
    i = jnp.arange(C, dtype=F32)
    diff = i[:, None] - i[None, :]
    lg = log_g[:, :, None, None]
    fwd = jnp.where(diff >= 0, jnp.exp(lg[0] * jnp.maximum(diff, 0.0)), 0.0)
    bwd = jnp.where(diff <= 0, jnp.exp(lg[1] * jnp.maximum(-diff, 0.0)), 0.0)
    rep = lambda a: jnp.repeat(a, RET_DK, axis=0)
    xi_f = jnp.exp(log_g[0][:, None] * (i + 1.0))
    xi_b = jnp.exp(log_g[1][:, None] * (C - i))
    ze_f = jnp.exp(log_g[0][:, None] * (C - 1.0 - i))
    ze_b = jnp.exp(log_g[1][:, None] * i)
    gc = jnp.exp(log_g * C)
    return {
        "dm": jnp.stack([fwd, bwd]),
        "xi": jnp.stack([rep(xi_f).T, rep(xi_b).T]),
        "zetaT": jnp.stack([rep(ze_f), rep(ze_b)]),
        "gc": jnp.broadcast_to(jnp.repeat(gc, RET_DK, axis=1)[:, :, None],
                               (2, RET_HEADS * RET_DK, RET_DV)),
    }


def _merge_kernel(oa_ref, ob_ref, oc_ref, ga_ref, gb_ref, gc_ref, wa_ref, wb_ref, wc_ref, m_ref):
    m = (jax.nn.sigmoid(ga_ref[...]) * jnp.dot(oa_ref[...], wa_ref[...], preferred_element_type=F32)
         + jax.nn.sigmoid(gb_ref[...]) * jnp.dot(ob_ref[...], wb_ref[...], preferred_element_type=F32)
         + jax.nn.sigmoid(gc_ref[...]) * jnp.dot(oc_ref[...], wc_ref[...], preferred_element_type=F32))
    m_ref[...] = m.astype(m_ref.dtype)


def _merge(oa, ob, oc, P, seg, wa, wb, wc, tm):
    n = oa.shape[0]
    D = wa.shape[1]
    tn = _tile(D, 512)
    nb = D // tn

    def rows(a):
        return pl.BlockSpec((tm, a.shape[1]), lambda i, j: (i, 0))

    def gate(name):
        return pl.BlockSpec((tm, tn), lambda i, j, o=seg[name][0] // tn: (i, o + j))

    def wcol(a):
        return pl.BlockSpec((a.shape[0], tn), lambda i, j: (0, j))

    return pl.pallas_call(
        _merge_kernel,
        name="merge",
        grid=(n // tm, nb),
        in_specs=[rows(oa), rows(ob), rows(oc), gate("ga"), gate("gb"), gate("gc"),
                  wcol(wa), wcol(wb), wcol(wc)],
        out_specs=pl.BlockSpec((tm, tn), lambda i, j: (i, j)),
        out_shape=jax.ShapeDtypeStruct((n, D), MXU),
        compiler_params=_params(("parallel", "arbitrary")),
    )(oa, ob, oc, P, P, P, wa, wb, wc)


def _layer_norm(y, g, b):
    mu = jnp.mean(y, axis=-1, keepdims=True)
    yc = y - mu
    return yc * lax.rsqrt(jnp.mean(yc * yc, axis=-1, keepdims=True) + NORM_EPS) * g + b


def _outln_kernel(m_ref, w_ref, x_ref, g1_ref, lg_ref, lb_ref, o_ref, *, alpha):
    o = jnp.dot(m_ref[...], w_ref[...], preferred_element_type=F32)
    o_ref[...] = _layer_norm(alpha * x_ref[...] + g1_ref[...] * o, lg_ref[...], lb_ref[...])


def _outln(m, w, xs, mod3, lg, lb, row_of_tile, alpha, tm):
    n, D = xs.shape
    vec = pl.BlockSpec((1, D), lambda i: (0, 0))
    return pl.pallas_call(
        functools.partial(_outln_kernel, alpha=alpha),
        name="outln",
        grid=(n // tm,),
        in_specs=[pl.BlockSpec((tm, D), lambda i: (i, 0)),
                  pl.BlockSpec((D, D), lambda i: (0, 0)),
                  pl.BlockSpec((tm, D), lambda i: (i, 0)),
                  pl.BlockSpec((None, 1, D), lambda i: (row_of_tile(i, tm), 0, 2)),
                  vec, vec],
        out_specs=pl.BlockSpec((tm, D), lambda i: (i, 0)),
        out_shape=jax.ShapeDtypeStruct((n, D), F32),
        compiler_params=_params(("parallel",)),
    )(m, w, xs, mod3, lg, lb)


def _bitonic_merge(xs):
    n = len(xs)
    if n == 1:
        return xs
    half = n // 2
    hi = [jnp.maximum(xs[i], xs[i + half]) for i in range(half)]
    lo = [jnp.minimum(xs[i], xs[i + half]) for i in range(half)]
    return _bitonic_merge(hi) + _bitonic_merge(lo)


def _sort_desc(xs):
    n = len(xs)
    if n == 1:
        return xs
    return _bitonic_merge(_sort_desc(xs[:n // 2]) + _sort_desc(xs[n // 2:])[::-1])


def _top16_of(groups):
    while len(groups) > 1:
        nxt = []
        for a, b in zip(groups[0::2], groups[1::2]):
            nxt.append(_bitonic_merge([jnp.maximum(a[i], b[PEER_TOPK - 1 - i]) for i in range(PEER_TOPK)]))
        groups = nxt
    return groups[0]


def _top16(xs):
    return _top16_of([_sort_desc(xs[g:g + PEER_TOPK]) for g in range(0, len(xs), PEER_TOPK)])


def _router_kernel(x_ref, sh_ref, sc_ref, wq_ref, k1_ref, k2_ref, k2h_ref,
                   ht_ref, thr_ref, e1_ref, s2_ref, e2_ref, s1_scr, s2_scr):
    K, H = PEER_N_KEYS, PEER_HEADS
    tm = x_ref.shape[0]
    h = x_ref[...] * (1.0 + sc_ref[...]) + sh_ref[...]
    hT = h.T.astype(MXU)
    ht_ref[...] = hT
    qT = jnp.dot(wq_ref[...], hT, preferred_element_type=F32).astype(MXU)
    s1_scr[...] = jnp.dot(k1_ref[...], qT, preferred_element_type=F32).reshape(K, H, tm)
    s2_scr[...] = jnp.dot(k2_ref[...], qT, preferred_element_type=F32).reshape(K, H, tm)
    s2_ref[...] = jnp.dot(k2h_ref[...], qT, preferred_element_type=F32).reshape(H, K, tm)

    def lane_block(c, _):
        ls = pl.ds(pl.multiple_of(c * LANES, LANES), LANES)
        s1 = s1_scr[:, :, ls]
        v1 = _top16([s1[k] for k in range(K)])
        v2 = _top16([s2_scr[k, :, ls] for k in range(K)])
        pairs = [(a, b) for a in range(PEER_TOPK) for b in range(PEER_TOPK)
                 if (a + 1) * (b + 1) <= PEER_TOPK]
        cand = [v1[a] + v2[b] for a, b in pairs]
        pad = [jnp.full_like(cand[0], -jnp.inf)] * (-len(cand) % PEER_TOPK)
        t = _top16(cand + pad)[-1]
        ex1 = [jnp.exp(v - v1[0]) for v in v1]
        ex2 = [jnp.exp(v - v2[0]) for v in v2]
        z = jnp.zeros_like(t)
        for (a, b), cv in zip(pairs, cand):
            z = z + jnp.where(cv >= t, ex1[a] * ex2[b], 0.0)
        inv_z = 1.0 / z
        thr = jnp.full(s1.shape, jnp.inf, F32)
        for b in range(PEER_TOPK):
            thr = jnp.where(s1 + v2[b][None] >= t[None], v2[b][None], thr)
        thr_ref[:, :, ls] = thr
        e1_ref[:, :, ls] = jnp.where(s1 >= v1[-1][None], jnp.exp(s1 - v1[0][None]), 0.0)
        for hh in range(H):
            s2h = s2_ref[hh, :, ls]
            e2_ref[hh, :, ls] = jnp.where(
                s2h >= v2[-1][hh:hh + 1], jnp.exp(s2h - v2[0][hh:hh + 1]) * inv_z[hh:hh + 1], 0.0)
        return 0

    lax.fori_loop(0, tm // LANES, lane_block, 0)


def _router(x1, mod3, wts, row_of_tile, tm):
    n, D = x1.shape
    K, H = PEER_N_KEYS, PEER_HEADS
    HQ = H * PEER_DQ

    def const(a):
        return pl.BlockSpec(a.shape, lambda i: (0,) * a.ndim)

    kh = pl.BlockSpec((K, H, tm), lambda i: (0, 0, i))
    hk = pl.BlockSpec((H, K, tm), lambda i: (0, 0, i))
    return pl.pallas_call(
        _router_kernel,
        name="router",
        grid=(n // tm,),
        in_specs=[pl.BlockSpec((tm, D), lambda i: (i, 0)),
                  pl.BlockSpec((None, 1, D), lambda i: (row_of_tile(i, tm), 0, 3)),
                  pl.BlockSpec((None, 1, D), lambda i: (row_of_tile(i, tm), 0, 4)),
                  const(wts["wqT"]), const(wts["k1kh"]), const(wts["k2kh"]), const(wts["k2hk"])],
        out_specs=[pl.BlockSpec((D, tm), lambda i: (0, i)), kh, kh, hk, hk],
        out_shape=[jax.ShapeDtypeStruct((D, n), MXU),
                   jax.ShapeDtypeStruct((K, H, n), F32), jax.ShapeDtypeStruct((K, H, n), F32),
                   jax.ShapeDtypeStruct((H, K, n), F32), jax.ShapeDtypeStruct((H, K, n), F32)],
        scratch_shapes=[pltpu.VMEM((K, H, tm), F32), pltpu.VMEM((K, H, tm), F32)],
        compiler_params=_params(("parallel",)),
    )(x1, mod3, mod3, wts["wqT"], wts["k1kh"], wts["k2kh"], wts["k2hk"])


PEER_ROWS = 32
PEER_TE = 512


def _peer_kernel(go_ref, ua_ref, ub_ref, u0_ref, vta_ref, vtb_ref, vtl_ref, ht_ref,
                 thr_ref, e1_ref, s2_ref, e2_ref, x_ref, g2_ref, lg_ref, lb_ref, o_ref,
                 acc, sc0, sc1, a0, a1, *, alpha):
    e = pl.program_id(1)
    last = pl.num_programs(1) - 1
    te, tm = sc0.shape
    K = PEER_N_KEYS
    nsub = te // K
    go1, go2 = go_ref[0] != 0, go_ref[1] != 0

    nval = nsub
    nsc = nsub
    dr = acc.shape[0] // nval

    def scores(u_tile_ref, sc):
        def piece(i):
            rows = slice(i * (te // nsc), (i + 1) * (te // nsc))
            sc[rows, :] = jnp.dot(u_tile_ref[rows, :], ht_ref[...], preferred_element_type=F32)
        return [functools.partial(piece, i) for i in range(nsc)]

    def values(vt_tile_ref, a):
        def piece(r):
            rows = slice(r * dr, (r + 1) * dr)
            acc[rows, :] += jnp.dot(vt_tile_ref[rows, :], a[...], preferred_element_type=F32)
        return [functools.partial(piece, r) for r in range(nval)]

    def gates(sc, a, tile):
        def piece(c, rb):
            ls = slice(c * LANES, (c + 1) * LANES)
            rs = slice(rb * PEER_ROWS, (rb + 1) * PEER_ROWS)
            thr = [thr_ref[tile * nsub + sub, :, ls] for sub in range(nsub)]
            e1r = [e1_ref[tile * nsub + sub, :, ls] for sub in range(nsub)]
            g = [None] * nsub
            for h in range(PEER_HEADS):
                s2p, e2p = s2_ref[h, rs, ls], e2_ref[h, rs, ls]
                for sub in range(nsub):
                    w = jnp.where(s2p >= thr[sub][h:h + 1], e2p, 0.0) * e1r[sub][h:h + 1]
                    g[sub] = w if g[sub] is None else g[sub] + w
            for sub in range(nsub):
                xs = slice(sub * K + rb * PEER_ROWS, sub * K + (rb + 1) * PEER_ROWS)
                x = sc[xs, ls]
                act = 0.5 * x * (1.0 + lax.erf(x * SQRT_HALF))
                a[xs, ls] = (act * g[sub]).astype(a.dtype)
        return [functools.partial(piece, c, rb) for c in range(tm // LANES) for rb in range(K // PEER_ROWS)]

    def interleave(matmul_pieces, vector_pieces):
        per = -(-len(vector_pieces) // len(matmul_pieces))
        for i, mm in enumerate(matmul_pieces):
            mm()
            for vp in vector_pieces[i * per:(i + 1) * per]:
                vp()

    def mix(xs, ys):
        if len(xs) < len(ys):
            xs, ys = ys, xs
        step = len(xs) // len(ys)
        out = []
        for i, x in enumerate(xs):
            out.append(x)
            if i % step == step - 1 and i // step < len(ys):
                out.append(ys[i // step])
        return out

    @pl.when(e == 0)
    def _():
        acc[...] = jnp.zeros_like(acc)
        a1[...] = jnp.zeros_like(a1)
        for p in scores(u0_ref, sc0):
            p()

    @pl.when(go1)
    def _():
        interleave(mix(scores(ub_ref, sc1), values(vtb_ref, a1)), gates(sc0, a0, 2 * e))

    @pl.when(go2)
    def _():
        interleave(mix(scores(ua_ref, sc0), values(vta_ref, a0)), gates(sc1, a1, 2 * e + 1))

    @pl.when(e == last)
    def _():
        for p in values(vtl_ref, a1):
            p()
        y = alpha * x_ref[...] + g2_ref[...] * acc[...].T
        o_ref[...] = _layer_norm(y, lg_ref[...], lb_ref[...])


def _peer(x1, routed, u, vt, mod3, lg, lb, row_of_tile, alpha, tm):
    n, D = x1.shape
    K, H = PEER_N_KEYS, PEER_HEADS
    te = PEER_TE
    nt = u.shape[0] // te
    ht, thr, e1, s2, e2 = routed
    once = dict(pipeline_mode=pl.Buffered(1))
    kh = pl.BlockSpec((K, H, tm), lambda i, e: (0, 0, i), **once)
    hk = pl.BlockSpec((H, K, tm), lambda i, e: (0, 0, i), **once)
    vec = pl.BlockSpec((1, D), lambda i, e: (0, 0))
    return pl.pallas_call(
        functools.partial(_peer_kernel, alpha=alpha),
        name="peer",
        grid=(n // tm, nt // 2),
        in_specs=[pl.BlockSpec(memory_space=pltpu.SMEM),
                  pl.BlockSpec((te, D), lambda i, e: (jnp.minimum(2 * e + 2, nt - 1), 0)),
                  pl.BlockSpec((te, D), lambda i, e: (2 * e + 1, 0)),
                  pl.BlockSpec((te, D), lambda i, e: (0, 0), **once),
                  pl.BlockSpec((D, te), lambda i, e: (0, 2 * e)),
                  pl.BlockSpec((D, te), lambda i, e: (0, jnp.maximum(2 * e - 1, 0))),
                  pl.BlockSpec((D, te), lambda i, e: (0, nt - 1), **once),
                  pl.BlockSpec((D, tm), lambda i, e: (0, i), **once),
                  kh, kh, hk, hk,
                  pl.BlockSpec((tm, D), lambda i, e: (i, 0), **once),
                  pl.BlockSpec((None, 1, D), lambda i, e: (row_of_tile(i, tm), 0, 5)),
                  vec, vec],
        out_specs=pl.BlockSpec((tm, D), lambda i, e: (i, 0)),
        out_shape=jax.ShapeDtypeStruct((n, D), F32),
        scratch_shapes=[pltpu.VMEM((D, tm), F32), pltpu.VMEM((te, tm), F32), pltpu.VMEM((te, tm), F32),
                        pltpu.VMEM((te, tm), MXU), pltpu.VMEM((te, tm), MXU)],
        compiler_params=_params(("arbitrary", "arbitrary"), 58),
    )(jnp.ones((2,), jnp.int32), u, u, u, vt, vt, vt, ht, thr, e1, s2, e2, x1, mod3, lg, lb)


def _pad_heads(w, heads, width, lo=0):
    lead = w.shape[:-1]
    w = w.reshape(*lead, heads, width)
    w = jnp.pad(w, [(0, 0)] * len(lead) + [(0, 0), (lo, LANES - lo - width)])
    return w.reshape(*lead, heads * LANES)


def _layer_weights(l, seg, total, w_in, mla_q_norm, mla_w_qup, mla_kv_norm, mla_w_kvup, gqa_q_norm,
                   gqa_k_norm, w_br_a, w_br_b, w_br_c, w_out, peer_w_q, peer_k1, peer_k2, peer_u, peer_v):
    D = w_in.shape[1]
    widths = (MLA_Q_LORA, MLA_KV_LORA, MLA_ROPE, RET_HEADS * RET_DK, RET_HEADS * RET_DK,
              RET_HEADS * RET_DV, RET_HEADS * RET_DV, GQA_HEADS * GQA_DH, GQA_KV_HEADS * GQA_DH,
              GQA_KV_HEADS * GQA_DH, D, D, D)
    names = ("cq", "ckv", "kr", "rq", "rk", "rv", "rg", "gq", "gk", "gv", "ga", "gb", "gc")
    cols = dict(zip(names, jnp.split(w_in[l], np.cumsum(widths)[:-1].tolist(), axis=1)))
    lo = MLA_NOPE
    cols["kr"] = jnp.pad(cols["kr"], ((0, 0), (lo, LANES - lo - MLA_ROPE)))
    order = sorted(seg, key=lambda name: seg[name][0])
    used = seg[order[-1]][0] + seg[order[-1]][1]
    win = jnp.concatenate([cols[name] for name in order] + [jnp.zeros((D, total - used), F32)],
                          axis=1).astype(MXU)
    kv = mla_w_kvup[l].reshape(MLA_KV_LORA, MLA_HEADS, MLA_NOPE + MLA_V)
    eye = jnp.eye(PEER_HEADS, dtype=F32)
    half = PEER_DQ // 2
    k1p = jnp.pad(peer_k1[l], ((0, 0), (0, half)))
    k2p = jnp.pad(peer_k2[l], ((0, 0), (half, 0)))
    HQ = PEER_HEADS * PEER_DQ
    return {
        "win": win,
        "qn": mla_q_norm[l][None], "kvn": mla_kv_norm[l][None],
        "wq": _pad_heads(mla_w_qup[l], MLA_HEADS, MLA_NOPE + MLA_ROPE).astype(MXU),
        "wk": _pad_heads(kv[..., :MLA_NOPE].reshape(MLA_KV_LORA, -1), MLA_HEADS, MLA_NOPE).astype(MXU),
        "wv": _pad_heads(kv[..., MLA_NOPE:].reshape(MLA_KV_LORA, -1), MLA_HEADS, MLA_V).astype(MXU),
        "gqn": gqa_q_norm[l][None], "gkn": gqa_k_norm[l][None],
        "wa": _pad_heads(w_br_a[l].T, MLA_HEADS, MLA_V).T.astype(MXU),
        "wb": w_br_b[l].astype(MXU), "wc": w_br_c[l].astype(MXU), "wo": w_out[l].astype(MXU),
        "wqT": peer_w_q[l].T.astype(MXU),
        "k1kh": jnp.einsum("kd,hg->khgd", k1p, eye).reshape(HQ, HQ).astype(MXU),
        "k2kh": jnp.einsum("kd,hg->khgd", k2p, eye).reshape(HQ, HQ).astype(MXU),
        "k2hk": jnp.einsum("kd,hg->hkgd", k2p, eye).reshape(HQ, HQ).astype(MXU),
        "u": peer_u[l].astype(MXU), "vt": peer_v[l].T.astype(MXU),
    }


def _rope_tables(seq):
    t = jnp.arange(seq, dtype=jnp.int32)
    row, col = (t // GRID_W).astype(F32), (t % GRID_W).astype(F32)

    def tab(r):
        nf = r // 4
        inv = ROPE_BASE ** (-jnp.arange(nf, dtype=F32) / nf)
        ar, ac = row[:, None] * inv[None], col[:, None] * inv[None]
        cos = jnp.concatenate([jnp.cos(ar)] * 2 + [jnp.cos(ac)] * 2, axis=1)
        sin = jnp.concatenate([-jnp.sin(ar), jnp.sin(ar), -jnp.sin(ac), jnp.sin(ac)], axis=1)
        return cos, sin

    ca, sa = tab(MLA_ROPE)
    lo, hi = MLA_NOPE, LANES - MLA_NOPE - MLA_ROPE
    ca = jnp.pad(ca, ((0, 0), (lo, hi)), constant_values=1.0)
    sa = jnp.pad(sa, ((0, 0), (lo, hi)))
    cb, sb = tab(RET_DK)
    cb, sb = jnp.tile(cb, (1, RET_HEADS)), jnp.tile(sb, (1, RET_HEADS))
    cc, sc = tab(GQA_DH)
    return [ca, sa, cb, sb, cc, sc]


def kernel(x, c, ctx, c_ctx, w_mod, b_mod, w_in, mla_q_norm, mla_w_qup, mla_kv_norm, mla_w_kvup, ret_decay_logit, gqa_q_norm, gqa_k_norm, w_br_a, w_br_b, w_br_c, w_out, ln1_g, ln1_b, peer_w_q, peer_k1, peer_k2, peer_u, peer_v, ln2_g, ln2_b):
    B, S, D = x.shape
    CTX = ctx.shape[1]
    L = w_in.shape[0]
    alpha = float((2.0 * L) ** 0.25)
    seg, total = _layout(D)

    tm_l = _tile(S, 512)
    tm_c = _tile(B * CTX, 512)
    tpb = S // tm_l
    lat_row = lambda i, tm: (i * tm) // S
    ctx_row = lambda i, tm: B
    lat_tab = lambda i: i % tpb
    ctx_tab = lambda i: 0

    nrow = -(-(B + 1) // 8) * 8
    cvec = jnp.concatenate([c, c_ctx[None], jnp.zeros((nrow - B - 1, D), F32)], axis=0)
    mod = _modulation(cvec, w_mod, b_mod)

    tabs_l = _rope_tables(S)
    tabs_c = [jnp.ones((tm_c, a.shape[1]), F32) if k % 2 == 0 else jnp.zeros((tm_c, a.shape[1]), F32)
              for k, a in enumerate(tabs_l)]

    xl = x.reshape(B * S, D)
    xc = ctx.reshape(B * CTX, D)
    for l in range(L):
        last = l == L - 1
        W = _layer_weights(l, seg, total, w_in, mla_q_norm, mla_w_qup, mla_kv_norm, mla_w_kvup,
                           gqa_q_norm, gqa_k_norm, w_br_a, w_br_b, w_br_c, w_out,
                           peer_w_q, peer_k1, peer_k2, peer_u, peer_v)
        dec = _decay_tables(ret_decay_logit[l])
        mod3 = mod[l].reshape(nrow, 1, 6 * D)
        lg1, lb1, lg2, lb2 = ln1_g[l][None], ln1_b[l][None], ln2_g[l][None], ln2_b[l][None]

        Pl = _inproj(xl, mod3, W["win"], lat_row, _tile(S, 1024))
        Pc = _inproj(xc, mod3, W["win"], ctx_row, tm_c)
        qa, ka, va, qb, kb, qc, kc, vc = _prep(Pl, seg, W, tabs_l, lat_tab, tm_l)
        qa_c, ka_c, va_c, qb_c, kb_c, qc_c, kc_c, vc_c = _prep(Pc, seg, W, tabs_c, ctx_tab, tm_c)

        oa = _attention(qa, ka_c, va_c, ka, va, B, 1, S)
        oc = _attention(qc, kc_c, vc_c, kc, vc, B, GQA_GROUP, S)
        r0 = jnp.zeros((B, 2, RET_HEADS * RET_DK, RET_DV), F32)
        ob_c, r_ctx = _retention(qb_c, kb_c, Pc, seg, dec, r0, B)
        ob, _ = _retention(qb, kb, Pl, seg, dec, r_ctx, B)

        def tail(xs, oa, ob, oc, P, row_of_tile, tm):
            m = _merge(oa, ob, oc, P, seg, W["wa"], W["wb"], W["wc"], tm)
            x1 = _outln(m, W["wo"], xs, mod3, lg1, lb1, row_of_tile, alpha, _tile(tm, 256))
            routed = _router(x1, mod3, W, row_of_tile, tm)
            return _peer(x1, routed, W["u"], W["vt"], mod3, lg2, lb2, row_of_tile, alpha, tm)

        if not last:
            oa_c = _attention(qa_c, ka_c, va_c, None, None, B, 1, CTX)
            oc_c = _attention(qc_c, kc_c, vc_c, None, None, B, GQA_GROUP, CTX)
            xc = tail(xc, oa_c, ob_c, oc_c, Pc, ctx_row, tm_c)
        xl = tail(xl, oa, ob, oc, Pl, lat_row, tm_l)
    return xl.reshape(B, S, D)
```

```python
import functools

import numpy as np
import jax
import jax.numpy as jnp
from jax import lax
from jax.experimental import pallas as pl
from jax.experimental.pallas import tpu as pltpu

GRID_W = 64
ROPE_BASE = 10000.0
NORM_EPS = 1e-5
RMS_EPS = 1e-6
MLA_HEADS, MLA_Q_LORA, MLA_KV_LORA, MLA_NOPE, MLA_ROPE, MLA_V = 8, 512, 256, 64, 32, 64
MLA_SCALE = (MLA_NOPE + MLA_ROPE) ** -0.5
RET_HEADS, RET_DK, RET_DV, RET_CHUNK = 8, 32, 64, 128
RET_K_SCALE = RET_DK ** -0.5
GQA_HEADS, GQA_KV_HEADS, GQA_DH = 8, 2, 128
GQA_GROUP = GQA_HEADS // GQA_KV_HEADS
GQA_SCALE = GQA_DH ** -0.5
PEER_HEADS, PEER_DQ, PEER_N_KEYS, PEER_TOPK = 8, 128, 128, 16
PEER_N_EXPERTS = PEER_N_KEYS * PEER_N_KEYS

LANES = 128
MXU = jnp.bfloat16
F32 = jnp.float32
NEG = -1e30
SQRT_HALF = float(np.sqrt(0.5))
LOG2E = float(np.log2(np.e))

_SEG = (("ga", None), ("gb", None), ("gc", None), ("gq", GQA_HEADS * GQA_DH),
        ("cq", MLA_Q_LORA), ("rv", RET_HEADS * RET_DV), ("rg", RET_HEADS * RET_DV),
        ("ckv", MLA_KV_LORA), ("rq", RET_HEADS * RET_DK), ("rk", RET_HEADS * RET_DK),
        ("gk", GQA_KV_HEADS * GQA_DH), ("gv", GQA_KV_HEADS * GQA_DH), ("kr", LANES))


def _layout(d_model):
    seg, off = {}, 0
    widths = [(name, d_model if w is None else w) for name, w in _SEG]
    for name, w in sorted(widths, key=lambda nw: -nw[1]):
        assert off % w == 0, (name, off, w)
        seg[name] = (off, w)
        off += w
    total = -(-off // 512) * 512
    return seg, total


def _params(sem, vmem_mb=48, flags=None):
    return pltpu.CompilerParams(dimension_semantics=sem, vmem_limit_bytes=vmem_mb << 20, flags=flags)


def _tile(n, pref):
    t = min(n, pref)
    while n % t:
        t //= 2
    return t


def _mod_kernel(c_ref, w_ref, b_ref, o_ref):
    c = c_ref[...]
    a = (c * jax.nn.sigmoid(c)).astype(MXU)
    o_ref[...] = jnp.dot(a, w_ref[...].astype(MXU), preferred_element_type=F32) + b_ref[...]


def _modulation(cvec, w_mod, b_mod):
    L, D, N6 = w_mod.shape
    R = cvec.shape[0]
    tn = _tile(N6, 1024)
    return pl.pallas_call(
        _mod_kernel,
        name="mod",
        grid=(L, N6 // tn),
        in_specs=[pl.BlockSpec((R, D), lambda l, j: (0, 0)),
                  pl.BlockSpec((None, D, tn), lambda l, j: (l, 0, j)),
                  pl.BlockSpec((None, 1, tn), lambda l, j: (l, 0, j))],
        out_specs=pl.BlockSpec((None, R, tn), lambda l, j: (l, 0, j)),
        out_shape=jax.ShapeDtypeStruct((L, R, N6), F32),
        compiler_params=_params(("parallel", "parallel")),
    )(cvec, w_mod, b_mod.reshape(L, 1, N6))


def _inproj_kernel(x_ref, sh_ref, sc_ref, w_ref, o_ref, h_scr):
    @pl.when(pl.program_id(1) == 0)
    def _():
        h_scr[...] = (x_ref[...] * (1.0 + sc_ref[...]) + sh_ref[...]).astype(h_scr.dtype)

    o_ref[...] = jnp.dot(h_scr[...], w_ref[...], preferred_element_type=F32)


def _inproj(xs, mod3, w, row_of_tile, tm):
    n, D = xs.shape
    NP = w.shape[1]
    tn = 512
    return pl.pallas_call(
        _inproj_kernel,
        name="inproj",
        grid=(n // tm, NP // tn),
        in_specs=[pl.BlockSpec((tm, D), lambda i, j: (i, 0)),
                  pl.BlockSpec((None, 1, D), lambda i, j: (row_of_tile(i, tm), 0, 0)),
                  pl.BlockSpec((None, 1, D), lambda i, j: (row_of_tile(i, tm), 0, 1)),
                  pl.BlockSpec((D, tn), lambda i, j: (0, j))],
        out_specs=pl.BlockSpec((tm, tn), lambda i, j: (i, j)),
        out_shape=jax.ShapeDtypeStruct((n, NP), F32),
        scratch_shapes=[pltpu.VMEM((tm, D), MXU)],
        compiler_params=_params(("parallel", "arbitrary")),
    )(xs, mod3, mod3, w)


def _swap_pairs(x, nf):
    lane = lax.broadcasted_iota(jnp.int32, x.shape, 1)
    up = pltpu.roll(x, LANES - nf, 1)
    down = pltpu.roll(x, nf, 1)
    return jnp.where((lane & nf) == 0, up, down)


def _rope(x, cos, sin, nf):
    return x * cos + _swap_pairs(x, nf) * sin


def _rms(x, g):
    return x * lax.rsqrt(jnp.mean(x * x, axis=-1, keepdims=True) + RMS_EPS) * g


def _prep_kernel(cq_ref, ckv_ref, kr_ref, rq_ref, rk_ref, gq_ref, gk_ref, gv_ref,
                 qn_ref, kvn_ref, wq_ref, wk_ref, wv_ref, gqn_ref, gkn_ref,
                 ca_ref, sa_ref, cb_ref, sb_ref, cc_ref, sc_ref,
                 qa_ref, ka_ref, va_ref, qb_ref, kb_ref, qc_ref, kc_ref, vc_ref):
    cqn = _rms(cq_ref[...], qn_ref[...]).astype(MXU)
    q = jnp.dot(cqn, wq_ref[...], preferred_element_type=F32)
    kvn = _rms(ckv_ref[...], kvn_ref[...]).astype(MXU)
    k = jnp.dot(kvn, wk_ref[...], preferred_element_type=F32)
    v = jnp.dot(kvn, wv_ref[...], preferred_element_type=F32)
    for h in range(MLA_HEADS):
        sl = slice(h * LANES, (h + 1) * LANES)
        va_ref[sl, :] = v[:, sl].T.astype(va_ref.dtype)
    ca, sa = ca_ref[...], sa_ref[...]
    kr = _rope(kr_ref[...], ca, sa, MLA_ROPE // 4)
    for h in range(MLA_HEADS):
        sl = slice(h * LANES, (h + 1) * LANES)
        qa_ref[:, sl] = (_rope(q[:, sl], ca, sa, MLA_ROPE // 4) * (MLA_SCALE * LOG2E)).astype(qa_ref.dtype)
        ka_ref[:, sl] = (k[:, sl] + kr).astype(ka_ref.dtype)
    for half in range(RET_HEADS * RET_DK // LANES):
        sl = slice(half * LANES, (half + 1) * LANES)
        cb, sb = cb_ref[:, sl], sb_ref[:, sl]
        qb_ref[:, sl] = _rope(rq_ref[:, sl], cb, sb, RET_DK // 4)
        kb_ref[:, sl] = _rope(rk_ref[:, sl], cb, sb, RET_DK // 4) * RET_K_SCALE
    cc, sc = cc_ref[...], sc_ref[...]
    for h in range(GQA_HEADS):
        sl = slice(h * LANES, (h + 1) * LANES)
        qc_ref[:, sl] = (_rope(_rms(gq_ref[:, sl], gqn_ref[...]), cc, sc, GQA_DH // 4)
                         * (GQA_SCALE * LOG2E)).astype(qc_ref.dtype)
    for h in range(GQA_KV_HEADS):
        sl = slice(h * LANES, (h + 1) * LANES)
        kc_ref[:, sl] = _rope(_rms(gk_ref[:, sl], gkn_ref[...]), cc, sc, GQA_DH // 4).astype(kc_ref.dtype)
    for h in range(GQA_KV_HEADS):
        sl = slice(h * LANES, (h + 1) * LANES)
        vc_ref[sl, :] = gv_ref[:, sl].T.astype(vc_ref.dtype)


def _prep(P, seg, wts, tabs, tab_of_tile, tm):
    n = P.shape[0]

    def pcol(name):
        off, w = seg[name]
        return pl.BlockSpec((tm, w), lambda i, o=off // w: (i, o))

    def const(a):
        return pl.BlockSpec(a.shape, lambda i: (0,) * a.ndim)

    def tab(a):
        return pl.BlockSpec((tm, a.shape[1]), lambda i: (tab_of_tile(i), 0))

    def rows(w):
        return pl.BlockSpec((tm, w), lambda i: (i, 0))

    consts = [wts["qn"], wts["kvn"], wts["wq"], wts["wk"], wts["wv"], wts["gqn"], wts["gkn"]]
    hq, hk = GQA_HEADS * LANES, GQA_KV_HEADS * LANES
    ha = MLA_HEADS * LANES
    hb = RET_HEADS * RET_DK
    outs = [(ha, MXU), (ha, MXU), (ha, MXU), (hb, F32), (hb, F32), (hq, MXU), (hk, MXU), (hk, MXU)]
    transposed = (2, 7)
    return pl.pallas_call(
        _prep_kernel,
        name="prep",
        grid=(n // tm,),
        in_specs=[pcol(s) for s in ("cq", "ckv", "kr", "rq", "rk", "gq", "gk", "gv")]
        + [const(a) for a in consts] + [tab(a) for a in tabs],
        out_specs=[pl.BlockSpec((w, tm), lambda i: (0, i)) if k in transposed else rows(w)
                   for k, (w, _) in enumerate(outs)],
        out_shape=[jax.ShapeDtypeStruct((w, n) if k in transposed else (n, w), dt)
                   for k, (w, dt) in enumerate(outs)],
        compiler_params=_params(("parallel",)),
    )(*([P] * 8), *consts, *tabs)


def _attn_kernel(*refs, group, tk, has_lat):
    if has_lat:
        q_ref, kc_ref, vc_ref, kl_ref, vl_ref, o_ref, acc_scr, s0_scr, s1_scr = refs
    else:
        q_ref, kc_ref, vc_ref, o_ref, acc_scr, s0_scr, s1_scr = refs
    s_scr = (s0_scr, s1_scr)
    tq = q_ref.shape[0]
    if group > 1:
        q = jnp.concatenate([q_ref[:, g * LANES:(g + 1) * LANES] for g in range(group)], axis=0)
    else:
        q = q_ref[...]
    rows = q.shape[0]
    chunks = [(kc_ref, vc_ref, 0, kc_ref.shape[0])]
    if has_lat:
        chunks += [(kl_ref, vl_ref, c * tk, tk) for c in range(kl_ref.shape[0] // tk)]

    NP = 1

    def scores(c, i):
        k_ref, _, off, size = chunks[c]
        r0, r1 = i * (size // NP), (i + 1) * (size // NP)
        s_scr[c % 2][r0:r1, :] = lax.dot_general(k_ref[off + r0:off + r1, :], q, (((1,), (1,)), ((), ())),
                                                 preferred_element_type=F32)

    def softmax_values(c, j, m, l):
        _, vt_ref, off, size = chunks[c]
        cols = slice(j * (rows // NP), (j + 1) * (rows // NP))
        s = s_scr[c % 2][:size, cols]
        m_new = jnp.maximum(m, jnp.max(s, axis=0, keepdims=True))
        a = jnp.exp2(m - m_new)
        p = jnp.exp2(s - m_new)
        acc_scr[:, cols] = a * acc_scr[:, cols] + jnp.dot(vt_ref[:, off:off + size], p.astype(MXU),
                                                          preferred_element_type=F32)
        return m_new, a * l + jnp.sum(p, axis=0, keepdims=True)

    acc_scr[...] = jnp.zeros_like(acc_scr)
    ml = [(jnp.full((1, rows // NP), NEG, F32), jnp.zeros((1, rows // NP), F32)) for _ in range(NP)]
    for i in range(NP):
        scores(0, i)
    for c in range(len(chunks)):
        for j in range(NP):
            if c + 1 < len(chunks):
                scores(c + 1, j)
            ml[j] = softmax_values(c, j, *ml[j])
    l = jnp.concatenate([l for _, l in ml], axis=1)
    o = acc_scr[...] * (1.0 / l)
    for g in range(group):
        o_ref[:, g * LANES:(g + 1) * LANES] = o[:, g * tq:(g + 1) * tq].T.astype(o_ref.dtype)


def _attention(q, kc, vct, kl, vlt, batch, group, rows_q):
    n, hq = q.shape
    hk = hq // (group * LANES)
    ctx_len = kc.shape[0] // batch
    has_lat = kl is not None
    tq = _tile(rows_q, 512 // group)
    nq = rows_q // tq
    in_specs = [pl.BlockSpec((tq, group * LANES), lambda b, h, i: (b * nq + i, h)),
                pl.BlockSpec((ctx_len, LANES), lambda b, h, i: (b, h)),
                pl.BlockSpec((LANES, ctx_len), lambda b, h, i: (h, b))]
    args = [q, kc, vct]
    tk = 512
    if has_lat:
        seq = kl.shape[0] // batch
        tk = _tile(seq, 1024)
        in_specs += [pl.BlockSpec((seq, LANES), lambda b, h, i: (b, h)),
                     pl.BlockSpec((LANES, seq), lambda b, h, i: (h, b))]
        args += [kl, vlt]
    return pl.pallas_call(
        functools.partial(_attn_kernel, group=group, tk=tk, has_lat=has_lat),
        name=f"attn_g{group}_{'lat' if has_lat else 'ctx'}",
        grid=(batch, hk, nq),
        in_specs=in_specs,
        out_specs=pl.BlockSpec((tq, group * LANES), lambda b, h, i: (b * nq + i, h)),
        out_shape=jax.ShapeDtypeStruct((n, hq), MXU),
        scratch_shapes=[pltpu.VMEM((LANES, group * tq), F32)]
        + [pltpu.VMEM((max(tk, ctx_len), group * tq), F32)] * 2,
        compiler_params=_params(("parallel", "parallel", "arbitrary")),
    )(*args)


def _ret_kernel(qf_ref, kf_ref, vf_ref, gf_ref, qb_ref, kb_ref, vb_ref, gb_ref,
                dm_ref, xi_ref, zeta_ref, gc_ref, avg_ref, r0_ref,
                of_ref, ob_ref, rT_ref, r_scr, yf_scr, yb_scr):
    j = pl.program_id(1)
    nch = pl.num_programs(1)
    C = RET_CHUNK

    @pl.when(j == 0)
    def _():
        r_scr[...] = r0_ref[...]

    def direction(d, q_ref, k_ref, v_ref):
        q = q_ref[...]
        k = k_ref[...]
        vb = v_ref[...].astype(MXU)
        qb = q.astype(MXU)
        qx = (q * xi_ref[d]).astype(MXU)
        kT = k.T
        kTb = kT.astype(MXU)
        kzT = (kT * zeta_ref[d]).astype(MXU)
        ys = []
        for h in range(RET_HEADS):
            ks = slice(h * RET_DK, (h + 1) * RET_DK)
            vs = slice(h * RET_DV, (h + 1) * RET_DV)
            r_h = r_scr[d, ks, :]
            s = jnp.dot(qb[:, ks], kTb[ks, :], preferred_element_type=F32) * dm_ref[d, h]
            y = (jnp.dot(s.astype(MXU), vb[:, vs], preferred_element_type=F32)
                 + jnp.dot(qx[:, ks], r_h.astype(MXU), preferred_element_type=F32))
            r_scr[d, ks, :] = gc_ref[d, ks, :] * r_h + jnp.dot(kzT[ks, :], vb[:, vs],
                                                               preferred_element_type=F32)
            ys.append(y)
        return ys

    ys_f = direction(0, qf_ref, kf_ref, vf_ref)
    ys_b = direction(1, qb_ref, kb_ref, vb_ref)
    row_f = pl.multiple_of(j * C, C)
    row_b = pl.multiple_of((nch - 1 - j) * C, C)

    y_f = jnp.concatenate(ys_f, axis=1)
    y_b = jnp.concatenate(ys_b, axis=1)

    @pl.when(2 * j < nch - 1)
    def _():
        yf_scr[pl.ds(row_f, C), :] = y_f
        yb_scr[pl.ds(row_b, C), :] = y_b

    def group_mean(a):
        hi = a.astype(MXU)
        lo = (a - hi.astype(F32)).astype(MXU)
        return (jnp.dot(hi, avg_ref[...], preferred_element_type=F32)
                + jnp.dot(lo, avg_ref[...], preferred_element_type=F32))

    def finish(y_here, other, row, g_ref, o_ref):
        y = y_here + other[pl.ds(row, C), :]
        yc = y - group_mean(y)
        yn = yc * lax.rsqrt(group_mean(yc * yc) + NORM_EPS)
        g = g_ref[...]
        o_ref[...] = (g * jax.nn.sigmoid(g) * yn).astype(o_ref.dtype)

    @pl.when(2 * j >= nch)
    def _():
        finish(y_f, yb_scr, row_f, gf_ref, of_ref)
        finish(y_b, yf_scr, row_b, gb_ref, ob_ref)

    @pl.when(j == nch - 1)
    def _():
        rT_ref[...] = r_scr[...]


def _retention(qb, kb, P, seg, dec, r0, batch):
    n = qb.shape[0]
    C = RET_CHUNK
    nch = n // batch // C
    assert nch % 2 == 0
    hk, hv = RET_HEADS * RET_DK, RET_HEADS * RET_DV
    cv, cg = seg["rv"][0] // hv, seg["rg"][0] // hv
    half = nch // 2

    fwd = lambda b, j: b * nch + j
    bwd = lambda b, j: b * nch + nch - 1 - j
    fwd_out = lambda b, j: b * nch + jnp.maximum(j, half)
    bwd_out = lambda b, j: b * nch + nch - 1 - jnp.maximum(j, half)
    whole = lambda a: pl.BlockSpec(a.shape, lambda b, j: (0,) * a.ndim)
    head_of = np.arange(hv) // RET_DV
    avg = jnp.asarray((head_of[:, None] == head_of[None, :]).astype(np.float32) / RET_DV, MXU)
    tabs = [dec["dm"], dec["xi"], dec["zetaT"], dec["gc"], avg]
    o_f, o_b, r_t = pl.pallas_call(
        _ret_kernel,
        name="retention",
        grid=(batch, nch),
        in_specs=[pl.BlockSpec((C, hk), lambda b, j: (fwd(b, j), 0)),
                  pl.BlockSpec((C, hk), lambda b, j: (fwd(b, j), 0)),
                  pl.BlockSpec((C, hv), lambda b, j: (fwd(b, j), cv)),
                  pl.BlockSpec((C, hv), lambda b, j: (fwd_out(b, j), cg)),
                  pl.BlockSpec((C, hk), lambda b, j: (bwd(b, j), 0)),
                  pl.BlockSpec((C, hk), lambda b, j: (bwd(b, j), 0)),
                  pl.BlockSpec((C, hv), lambda b, j: (bwd(b, j), cv)),
                  pl.BlockSpec((C, hv), lambda b, j: (bwd_out(b, j), cg))]
        + [whole(t) for t in tabs]
        + [pl.BlockSpec((None, 2, hk, RET_DV), lambda b, j: (b, 0, 0, 0))],
        out_specs=[pl.BlockSpec((C, hv), lambda b, j: (fwd_out(b, j), 0)),
                   pl.BlockSpec((C, hv), lambda b, j: (bwd_out(b, j), 0)),
                   pl.BlockSpec((None, 2, hk, RET_DV), lambda b, j: (b, 0, 0, 0))],
        out_shape=[jax.ShapeDtypeStruct((n, hv), MXU), jax.ShapeDtypeStruct((n, hv), MXU),
                   jax.ShapeDtypeStruct((batch, 2, hk, RET_DV), F32)],
        scratch_shapes=[pltpu.VMEM((2, hk, RET_DV), F32), pltpu.VMEM((nch * C, hv), F32),
                        pltpu.VMEM((nch * C, hv), F32)],
        compiler_params=_params(("parallel", "arbitrary")),
    )(qb, kb, P, P, qb, kb, P, P, *tabs, r0)
    o_f = o_f.reshape(batch, 2, half * C, hv)
    o_b = o_b.reshape(batch, 2, half * C, hv)
    return jnp.stack([o_b[:, 0], o_f[:, 1]], axis=1).reshape(n, hv), r_t


def _decay_tables(logit):
    C = RET_CHUNK
    log_g = jax.nn.log_sigmoid(logit.astype(F32))
    i = jnp.arange(C, dtype=F32)
    diff = i[:, None] - i[None, :]
    lg = log_g[:, :, None, None]
    fwd = jnp.where(diff >= 0, jnp.exp(lg[0] * jnp.maximum(diff, 0.0)), 0.0)
    bwd = jnp.where(diff <= 0, jnp.exp(lg[1] * jnp.maximum(-diff, 0.0)), 0.0)
    rep = lambda a: jnp.repeat(a, RET_DK, axis=0)
    xi_f = jnp.exp(log_g[0][:, None] * (i + 1.0))
    xi_b = jnp.exp(log_g[1][:, None] * (C - i))
    ze_f = jnp.exp(log_g[0][:, None] * (C - 1.0 - i))
    ze_b = jnp.exp(log_g[1][:, None] * i)
    gc = jnp.exp(log_g * C)
    return {
        "dm": jnp.stack([fwd, bwd]),
        "xi": jnp.stack([rep(xi_f).T, rep(xi_b).T]),
        "zetaT": jnp.stack([rep(ze_f), rep(ze_b)]),
        "gc": jnp.broadcast_to(jnp.repeat(gc, RET_DK, axis=1)[:, :, None],
                               (2, RET_HEADS * RET_DK, RET_DV)),
    }


def _merge_kernel(oa_ref, ob_ref, oc_ref, ga_ref, gb_ref, gc_ref, wa_ref, wb_ref, wc_ref, m_ref):
    m = (jax.nn.sigmoid(ga_ref[...]) * jnp.dot(oa_ref[...], wa_ref[...], preferred_element_type=F32)
         + jax.nn.sigmoid(gb_ref[...]) * jnp.dot(ob_ref[...], wb_ref[...], preferred_element_type=F32)
         + jax.nn.sigmoid(gc_ref[...]) * jnp.dot(oc_ref[...], wc_ref[...], preferred_element_type=F32))
    m_ref[...] = m.astype(m_ref.dtype)


def _merge(oa, ob, oc, P, seg, wa, wb, wc, tm):
    n = oa.shape[0]
    D = wa.shape[1]
    tn = _tile(D, 512)
    nb = D // tn

    def rows(a):
        return pl.BlockSpec((tm, a.shape[1]), lambda i, j: (i, 0))

    def gate(name):
        return pl.BlockSpec((tm, tn), lambda i, j, o=seg[name][0] // tn: (i, o + j))

    def wcol(a):
        return pl.BlockSpec((a.shape[0], tn), lambda i, j: (0, j))

    return pl.pallas_call(
        _merge_kernel,
        name="merge",
        grid=(n // tm, nb),
        in_specs=[rows(oa), rows(ob), rows(oc), gate("ga"), gate("gb"), gate("gc"),
                  wcol(wa), wcol(wb), wcol(wc)],
        out_specs=pl.BlockSpec((tm, tn), lambda i, j: (i, j)),
        out_shape=jax.ShapeDtypeStruct((n, D), MXU),
        compiler_params=_params(("parallel", "arbitrary")),
    )(oa, ob, oc, P, P, P, wa, wb, wc)


def _layer_norm(y, g, b):
    mu = jnp.mean(y, axis=-1, keepdims=True)
    yc = y - mu
    return yc * lax.rsqrt(jnp.mean(yc * yc, axis=-1, keepdims=True) + NORM_EPS) * g + b


def _outln_kernel(m_ref, w_ref, x_ref, g1_ref, lg_ref, lb_ref, o_ref, *, alpha):
    o = jnp.dot(m_ref[...], w_ref[...], preferred_element_type=F32)
    o_ref[...] = _layer_norm(alpha * x_ref[...] + g1_ref[...] * o, lg_ref[...], lb_ref[...])


def _outln(m, w, xs, mod3, lg, lb, row_of_tile, alpha, tm):
    n, D = xs.shape
    vec = pl.BlockSpec((1, D), lambda i: (0, 0))
    return pl.pallas_call(
        functools.partial(_outln_kernel, alpha=alpha),
        name="outln",
        grid=(n // tm,),
        in_specs=[pl.BlockSpec((tm, D), lambda i: (i, 0)),
                  pl.BlockSpec((D, D), lambda i: (0, 0)),
                  pl.BlockSpec((tm, D), lambda i: (i, 0)),
                  pl.BlockSpec((None, 1, D), lambda i: (row_of_tile(i, tm), 0, 2)),
                  vec, vec],
        out_specs=pl.BlockSpec((tm, D), lambda i: (i, 0)),
        out_shape=jax.ShapeDtypeStruct((n, D), F32),
        compiler_params=_params(("parallel",)),
    )(m, w, xs, mod3, lg, lb)


def _bitonic_merge(xs):
    n = len(xs)
    if n == 1:
        return xs
    half = n // 2
    hi = [jnp.maximum(xs[i], xs[i + half]) for i in range(half)]
    lo = [jnp.minimum(xs[i], xs[i + half]) for i in range(half)]
    return _bitonic_merge(hi) + _bitonic_merge(lo)


def _sort_desc(xs):
    n = len(xs)
    if n == 1:
        return xs
    return _bitonic_merge(_sort_desc(xs[:n // 2]) + _sort_desc(xs[n // 2:])[::-1])


def _top16_of(groups):
    while len(groups) > 1:
        nxt = []
        for a, b in zip(groups[0::2], groups[1::2]):
            nxt.append(_bitonic_merge([jnp.maximum(a[i], b[PEER_TOPK - 1 - i]) for i in range(PEER_TOPK)]))
        groups = nxt
    return groups[0]


def _top16(xs):
    return _top16_of([_sort_desc(xs[g:g + PEER_TOPK]) for g in range(0, len(xs), PEER_TOPK)])


def _router_kernel(x_ref, sh_ref, sc_ref, wq_ref, k1_ref, k2_ref, k2h_ref,
                   ht_ref, thr_ref, e1_ref, s2_ref, e2_ref, s1_scr, s2_scr):
    K, H = PEER_N_KEYS, PEER_HEADS
    tm = x_ref.shape[0]
    h = x_ref[...] * (1.0 + sc_ref[...]) + sh_ref[...]
    hT = h.T.astype(MXU)
    ht_ref[...] = hT
    qT = jnp.dot(wq_ref[...], hT, preferred_element_type=F32).astype(MXU)
    s1_scr[...] = jnp.dot(k1_ref[...], qT, preferred_element_type=F32).reshape(K, H, tm)
    s2_scr[...] = jnp.dot(k2_ref[...], qT, preferred_element_type=F32).reshape(K, H, tm)
    s2_ref[...] = jnp.dot(k2h_ref[...], qT, preferred_element_type=F32).reshape(H, K, tm)

    def lane_block(c, _):
        ls = pl.ds(pl.multiple_of(c * LANES, LANES), LANES)
        s1 = s1_scr[:, :, ls]
        v1 = _top16([s1[k] for k in range(K)])
        v2 = _top16([s2_scr[k, :, ls] for k in range(K)])
        pairs = [(a, b) for a in range(PEER_TOPK) for b in range(PEER_TOPK)
                 if (a + 1) * (b + 1) <= PEER_TOPK]
        cand = [v1[a] + v2[b] for a, b in pairs]
        pad = [jnp.full_like(cand[0], -jnp.inf)] * (-len(cand) % PEER_TOPK)
        t = _top16(cand + pad)[-1]
        ex1 = [jnp.exp(v - v1[0]) for v in v1]
        ex2 = [jnp.exp(v - v2[0]) for v in v2]
        z = jnp.zeros_like(t)
        for (a, b), cv in zip(pairs, cand):
            z = z + jnp.where(cv >= t, ex1[a] * ex2[b], 0.0)
        inv_z = 1.0 / z
        thr = jnp.full(s1.shape, jnp.inf, F32)
        for b in range(PEER_TOPK):
            thr = jnp.where(s1 + v2[b][None] >= t[None], v2[b][None], thr)
        thr_ref[:, :, ls] = thr
        e1_ref[:, :, ls] = jnp.where(s1 >= v1[-1][None], jnp.exp(s1 - v1[0][None]), 0.0)
        for hh in range(H):
            s2h = s2_ref[hh, :, ls]
            e2_ref[hh, :, ls] = jnp.where(
                s2h >= v2[-1][hh:hh + 1], jnp.exp(s2h - v2[0][hh:hh + 1]) * inv_z[hh:hh + 1], 0.0)
        return 0

    lax.fori_loop(0, tm // LANES, lane_block, 0)


def _router(x1, mod3, wts, row_of_tile, tm):
    n, D = x1.shape
    K, H = PEER_N_KEYS, PEER_HEADS
    HQ = H * PEER_DQ

    def const(a):
        return pl.BlockSpec(a.shape, lambda i: (0,) * a.ndim)

    kh = pl.BlockSpec((K, H, tm), lambda i: (0, 0, i))
    hk = pl.BlockSpec((H, K, tm), lambda i: (0, 0, i))
    return pl.pallas_call(
        _router_kernel,
        name="router",
        grid=(n // tm,),
        in_specs=[pl.BlockSpec((tm, D), lambda i: (i, 0)),
                  pl.BlockSpec((None, 1, D), lambda i: (row_of_tile(i, tm), 0, 3)),
                  pl.BlockSpec((None, 1, D), lambda i: (row_of_tile(i, tm), 0, 4)),
                  const(wts["wqT"]), const(wts["k1kh"]), const(wts["k2kh"]), const(wts["k2hk"])],
        out_specs=[pl.BlockSpec((D, tm), lambda i: (0, i)), kh, kh, hk, hk],
        out_shape=[jax.ShapeDtypeStruct((D, n), MXU),
                   jax.ShapeDtypeStruct((K, H, n), F32), jax.ShapeDtypeStruct((K, H, n), F32),
                   jax.ShapeDtypeStruct((H, K, n), F32), jax.ShapeDtypeStruct((H, K, n), F32)],
        scratch_shapes=[pltpu.VMEM((K, H, tm), F32), pltpu.VMEM((K, H, tm), F32)],
        compiler_params=_params(("parallel",)),
    )(x1, mod3, mod3, wts["wqT"], wts["k1kh"], wts["k2kh"], wts["k2hk"])


PEER_ROWS = 32
PEER_TE = 512


def _peer_kernel(go_ref, ua_ref, ub_ref, u0_ref, vta_ref, vtb_ref, vtl_ref, ht_ref,
                 thr_ref, e1_ref, s2_ref, e2_ref, x_ref, g2_ref, lg_ref, lb_ref, o_ref,
                 acc, sc0, sc1, a0, a1, *, alpha):
    e = pl.program_id(1)
    last = pl.num_programs(1) - 1
    te, tm = sc0.shape
    K = PEER_N_KEYS
    nsub = te // K
    go1, go2 = go_ref[0] != 0, go_ref[1] != 0

    nval = nsub
    nsc = nsub
    dr = acc.shape[0] // nval

    def scores(u_tile_ref, sc):
        def piece(i):
            rows = slice(i * (te // nsc), (i + 1) * (te // nsc))
            sc[rows, :] = jnp.dot(u_tile_ref[rows, :], ht_ref[...], preferred_element_type=F32)
        return [functools.partial(piece, i) for i in range(nsc)]

    def values(vt_tile_ref, a):
        def piece(r):
            rows = slice(r * dr, (r + 1) * dr)
            acc[rows, :] += jnp.dot(vt_tile_ref[rows, :], a[...], preferred_element_type=F32)
        return [functools.partial(piece, r) for r in range(nval)]

    def gates(sc, a, tile):
        def piece(c, rb):
            ls = slice(c * LANES, (c + 1) * LANES)
            rs = slice(rb * PEER_ROWS, (rb + 1) * PEER_ROWS)
            thr = [thr_ref[tile * nsub + sub, :, ls] for sub in range(nsub)]
            e1r = [e1_ref[tile * nsub + sub, :, ls] for sub in range(nsub)]
            g = [None] * nsub
            for h in range(PEER_HEADS):
                s2p, e2p = s2_ref[h, rs, ls], e2_ref[h, rs, ls]
                for sub in range(nsub):
                    w = jnp.where(s2p >= thr[sub][h:h + 1], e2p, 0.0) * e1r[sub][h:h + 1]
                    g[sub] = w if g[sub] is None else g[sub] + w
            for sub in range(nsub):
                xs = slice(sub * K + rb * PEER_ROWS, sub * K + (rb + 1) * PEER_ROWS)
                x = sc[xs, ls]
                act = 0.5 * x * (1.0 + lax.erf(x * SQRT_HALF))
                a[xs, ls] = (act * g[sub]).astype(a.dtype)
        return [functools.partial(piece, c, rb) for c in range(tm // LANES) for rb in range(K // PEER_ROWS)]

    def interleave(matmul_pieces, vector_pieces):
        per = -(-len(vector_pieces) // len(matmul_pieces))
        for i, mm in enumerate(matmul_pieces):
            mm()
            for vp in vector_pieces[i * per:(i + 1) * per]:
                vp()

    def mix(xs, ys):
        if len(xs) < len(ys):
            xs, ys = ys, xs
        step = len(xs) // len(ys)
        out = []
        for i, x in enumerate(xs):
            out.append(x)
            if i % step == step - 1 and i // step < len(ys):
                out.append(ys[i // step])
        return out

    @pl.when(e == 0)
    def _():
        acc[...] = jnp.zeros_like(acc)
        a1[...] = jnp.zeros_like(a1)
        for p in scores(u0_ref, sc0):
            p()

    @pl.when(go1)
    def _():
        interleave(mix(scores(ub_ref, sc1), values(vtb_ref, a1)), gates(sc0, a0, 2 * e))

    @pl.when(go2)
    def _():
        interleave(mix(scores(ua_ref, sc0), values(vta_ref, a0)), gates(sc1, a1, 2 * e + 1))

    @pl.when(e == last)
    def _():
        for p in values(vtl_ref, a1):
            p()
        y = alpha * x_ref[...] + g2_ref[...] * acc[...].T
        o_ref[...] = _layer_norm(y, lg_ref[...], lb_ref[...])


def _peer(x1, routed, u, vt, mod3, lg, lb, row_of_tile, alpha, tm):
    n, D = x1.shape
    K, H = PEER_N_KEYS, PEER_HEADS
    te = PEER_TE
    nt = u.shape[0] // te
    ht, thr, e1, s2, e2 = routed
    once = dict(pipeline_mode=pl.Buffered(1))
    kh = pl.BlockSpec((K, H, tm), lambda i, e: (0, 0, i), **once)
    hk = pl.BlockSpec((H, K, tm), lambda i, e: (0, 0, i), **once)
    vec = pl.BlockSpec((1, D), lambda i, e: (0, 0))
    return pl.pallas_call(
        functools.partial(_peer_kernel, alpha=alpha),
        name="peer",
        grid=(n // tm, nt // 2),
        in_specs=[pl.BlockSpec(memory_space=pltpu.SMEM),
                  pl.BlockSpec((te, D), lambda i, e: (jnp.minimum(2 * e + 2, nt - 1), 0)),
                  pl.BlockSpec((te, D), lambda i, e: (2 * e + 1, 0)),
                  pl.BlockSpec((te, D), lambda i, e: (0, 0), **once),
                  pl.BlockSpec((D, te), lambda i, e: (0, 2 * e)),
                  pl.BlockSpec((D, te), lambda i, e: (0, jnp.maximum(2 * e - 1, 0))),
                  pl.BlockSpec((D, te), lambda i, e: (0, nt - 1), **once),
                  pl.BlockSpec((D, tm), lambda i, e: (0, i), **once),
                  kh, kh, hk, hk,
                  pl.BlockSpec((tm, D), lambda i, e: (i, 0), **once),
                  pl.BlockSpec((None, 1, D), lambda i, e: (row_of_tile(i, tm), 0, 5)),
                  vec, vec],
        out_specs=pl.BlockSpec((tm, D), lambda i, e: (i, 0)),
        out_shape=jax.ShapeDtypeStruct((n, D), F32),
        scratch_shapes=[pltpu.VMEM((D, tm), F32), pltpu.VMEM((te, tm), F32), pltpu.VMEM((te, tm), F32),
                        pltpu.VMEM((te, tm), MXU), pltpu.VMEM((te, tm), MXU)],
        compiler_params=_params(("arbitrary", "arbitrary"), 58),
    )(jnp.ones((2,), jnp.int32), u, u, u, vt, vt, vt, ht, thr, e1, s2, e2, x1, mod3, lg, lb)


def _pad_heads(w, heads, width, lo=0):
    lead = w.shape[:-1]
    w = w.reshape(*lead, heads, width)
    w = jnp.pad(w, [(0, 0)] * len(lead) + [(0, 0), (lo, LANES - lo - width)])
    return w.reshape(*lead, heads * LANES)


def _layer_weights(l, seg, total, w_in, mla_q_norm, mla_w_qup, mla_kv_norm, mla_w_kvup, gqa_q_norm,
                   gqa_k_norm, w_br_a, w_br_b, w_br_c, w_out, peer_w_q, peer_k1, peer_k2, peer_u, peer_v):
    D = w_in.shape[1]
    widths = (MLA_Q_LORA, MLA_KV_LORA, MLA_ROPE, RET_HEADS * RET_DK, RET_HEADS * RET_DK,
              RET_HEADS * RET_DV, RET_HEADS * RET_DV, GQA_HEADS * GQA_DH, GQA_KV_HEADS * GQA_DH,
              GQA_KV_HEADS * GQA_DH, D, D, D)
    names = ("cq", "ckv", "kr", "rq", "rk", "rv", "rg", "gq", "gk", "gv", "ga", "gb", "gc")
    cols = dict(zip(names, jnp.split(w_in[l], np.cumsum(widths)[:-1].tolist(), axis=1)))
    lo = MLA_NOPE
    cols["kr"] = jnp.pad(cols["kr"], ((0, 0), (lo, LANES - lo - MLA_ROPE)))
    order = sorted(seg, key=lambda name: seg[name][0])
    used = seg[order[-1]][0] + seg[order[-1]][1]
    win = jnp.concatenate([cols[name] for name in order] + [jnp.zeros((D, total - used), F32)],
                          axis=1).astype(MXU)
    kv = mla_w_kvup[l].reshape(MLA_KV_LORA, MLA_HEADS, MLA_NOPE + MLA_V)
    eye = jnp.eye(PEER_HEADS, dtype=F32)
    half = PEER_DQ // 2
    k1p = jnp.pad(peer_k1[l], ((0, 0), (0, half)))
    k2p = jnp.pad(peer_k2[l], ((0, 0), (half, 0)))
    HQ = PEER_HEADS * PEER_DQ
    return {
        "win": win,
        "qn": mla_q_norm[l][None], "kvn": mla_kv_norm[l][None],
        "wq": _pad_heads(mla_w_qup[l], MLA_HEADS, MLA_NOPE + MLA_ROPE).astype(MXU),
        "wk": _pad_heads(kv[..., :MLA_NOPE].reshape(MLA_KV_LORA, -1), MLA_HEADS, MLA_NOPE).astype(MXU),
        "wv": _pad_heads(kv[..., MLA_NOPE:].reshape(MLA_KV_LORA, -1), MLA_HEADS, MLA_V).astype(MXU),
        "gqn": gqa_q_norm[l][None], "gkn": gqa_k_norm[l][None],
        "wa": _pad_heads(w_br_a[l].T, MLA_HEADS, MLA_V).T.astype(MXU),
        "wb": w_br_b[l].astype(MXU), "wc": w_br_c[l].astype(MXU), "wo": w_out[l].astype(MXU),
        "wqT": peer_w_q[l].T.astype(MXU),
        "k1kh": jnp.einsum("kd,hg->khgd", k1p, eye).reshape(HQ, HQ).astype(MXU),
        "k2kh": jnp.einsum("kd,hg->khgd", k2p, eye).reshape(HQ, HQ).astype(MXU),
        "k2hk": jnp.einsum("kd,hg->hkgd", k2p, eye).reshape(HQ, HQ).astype(MXU),
        "u": peer_u[l].astype(MXU), "vt": peer_v[l].T.astype(MXU),
    }


def _rope_tables(seq):
    t = jnp.arange(seq, dtype=jnp.int32)
    row, col = (t // GRID_W).astype(F32), (t % GRID_W).astype(F32)

    def tab(r):
        nf = r // 4
        inv = ROPE_BASE ** (-jnp.arange(nf, dtype=F32) / nf)
        ar, ac = row[:, None] * inv[None], col[:, None] * inv[None]
        cos = jnp.concatenate([jnp.cos(ar)] * 2 + [jnp.cos(ac)] * 2, axis=1)
        sin = jnp.concatenate([-jnp.sin(ar), jnp.sin(ar), -jnp.sin(ac), jnp.sin(ac)], axis=1)
        return cos, sin

    ca, sa = tab(MLA_ROPE)
    lo, hi = MLA_NOPE, LANES - MLA_NOPE - MLA_ROPE
    ca = jnp.pad(ca, ((0, 0), (lo, hi)), constant_values=1.0)
    sa = jnp.pad(sa, ((0, 0), (lo, hi)))
    cb, sb = tab(RET_DK)
    cb, sb = jnp.tile(cb, (1, RET_HEADS)), jnp.tile(sb, (1, RET_HEADS))
    cc, sc = tab(GQA_DH)
    return [ca, sa, cb, sb, cc, sc]


def kernel(x, c, ctx, c_ctx, w_mod, b_mod, w_in, mla_q_norm, mla_w_qup, mla_kv_norm, mla_w_kvup, ret_decay_logit, gqa_q_norm, gqa_k_norm, w_br_a, w_br_b, w_br_c, w_out, ln1_g, ln1_b, peer_w_q, peer_k1, peer_k2, peer_u, peer_v, ln2_g, ln2_b):
    B, S, D = x.shape
    CTX = ctx.shape[1]
    L = w_in.shape[0]
    alpha = float((2.0 * L) ** 0.25)
    seg, total = _layout(D)

    tm_l = _tile(S, 512)
    tm_c = _tile(B * CTX, 512)
    tpb = S // tm_l
    lat_row = lambda i, tm: (i * tm) // S
    ctx_row = lambda i, tm: B
    lat_tab = lambda i: i % tpb
    ctx_tab = lambda i: 0

    nrow = -(-(B + 1) // 8) * 8
    cvec = jnp.concatenate([c, c_ctx[None], jnp.zeros((nrow - B - 1, D), F32)], axis=0)
    mod = _modulation(cvec, w_mod, b_mod)

    tabs_l = _rope_tables(S)
    tabs_c = [jnp.ones((tm_c, a.shape[1]), F32) if k % 2 == 0 else jnp.zeros((tm_c, a.shape[1]), F32)
              for k, a in enumerate(tabs_l)]

    xl = x.reshape(B * S, D)
    xc = ctx.reshape(B * CTX, D)
    for l in range(L):
        last = l == L - 1
        W = _layer_weights(l, seg, total, w_in, mla_q_norm, mla_w_qup, mla_kv_norm, mla_w_kvup,
                           gqa_q_norm, gqa_k_norm, w_br_a, w_br_b, w_br_c, w_out,
                           peer_w_q, peer_k1, peer_k2, peer_u, peer_v)
        dec = _decay_tables(ret_decay_logit[l])
        mod3 = mod[l].reshape(nrow, 1, 6 * D)
        lg1, lb1, lg2, lb2 = ln1_g[l][None], ln1_b[l][None], ln2_g[l][None], ln2_b[l][None]

        Pl = _inproj(xl, mod3, W["win"], lat_row, _tile(S, 1024))
        Pc = _inproj(xc, mod3, W["win"], ctx_row, tm_c)
        qa, ka, va, qb, kb, qc, kc, vc = _prep(Pl, seg, W, tabs_l, lat_tab, tm_l)
        qa_c, ka_c, va_c, qb_c, kb_c, qc_c, kc_c, vc_c = _prep(Pc, seg, W, tabs_c, ctx_tab, tm_c)

        oa = _attention(qa, ka_c, va_c, ka, va, B, 1, S)
        oc = _attention(qc, kc_c, vc_c, kc, vc, B, GQA_GROUP, S)
        r0 = jnp.zeros((B, 2, RET_HEADS * RET_DK, RET_DV), F32)
        ob_c, r_ctx = _retention(qb_c, kb_c, Pc, seg, dec, r0, B)
        ob, _ = _retention(qb, kb, Pl, seg, dec, r_ctx, B)

        def tail(xs, oa, ob, oc, P, row_of_tile, tm):
            m = _merge(oa, ob, oc, P, seg, W["wa"], W["wb"], W["wc"], _tile(oa.shape[0], 1024))
            x1 = _outln(m, W["wo"], xs, mod3, lg1, lb1, row_of_tile, alpha, _tile(tm, 256))
            routed = _router(x1, mod3, W, row_of_tile, tm)
            return _peer(x1, routed, W["u"], W["vt"], mod3, lg2, lb2, row_of_tile, alpha, tm)

        if not last:
            oa_c = _attention(qa_c, ka_c, va_c, None, None, B, 1, CTX)
            oc_c = _attention(qc_c, kc_c, vc_c, None, None, B, GQA_GROUP, CTX)
            xc = tail(xc, oa_c, ob_c, oc_c, Pc, ctx_row, tm_c)
        xl = tail(xl, oa, ob, oc, Pl, lat_row, tm_l)
    return xl.reshape(B, S, D)
```

```python
import functools

import numpy as np
import jax
import jax.numpy as jnp
from jax import lax
from jax.experimental import pallas as pl
from jax.experimental.pallas import tpu as pltpu

GRID_W = 64
ROPE_BASE = 10000.0
NORM_EPS = 1e-5
RMS_EPS = 1e-6
MLA_HEADS, MLA_Q_LORA, MLA_KV_LORA, MLA_NOPE, MLA_ROPE, MLA_V = 8, 512, 256, 64, 32, 64
MLA_SCALE = (MLA_NOPE + MLA_ROPE) ** -0.5
RET_HEADS, RET_DK, RET_DV, RET_CHUNK = 8, 32, 64, 128
RET_K_SCALE = RET_DK ** -0.5
GQA_HEADS, GQA_KV_HEADS, GQA_DH = 8, 2, 128
GQA_GROUP = GQA_HEADS // GQA_KV_HEADS
GQA_SCALE = GQA_DH ** -0.5
PEER_HEADS, PEER_DQ, PEER_N_KEYS, PEER_TOPK = 8, 128, 128, 16
PEER_N_EXPERTS = PEER_N_KEYS * PEER_N_KEYS

LANES = 128
MXU = jnp.bfloat16
F32 = jnp.float32
NEG = -1e30
SQRT_HALF = float(np.sqrt(0.5))
LOG2E = float(np.log2(np.e))

_SEG = (("ga", None), ("gb", None), ("gc", None), ("gq", GQA_HEADS * GQA_DH),
        ("cq", MLA_Q_LORA), ("rv", RET_HEADS * RET_DV), ("rg", RET_HEADS * RET_DV),
        ("ckv", MLA_KV_LORA), ("rq", RET_HEADS * RET_DK), ("rk", RET_HEADS * RET_DK),
        ("gk", GQA_KV_HEADS * GQA_DH), ("gv", GQA_KV_HEADS * GQA_DH), ("kr", LANES))


def _layout(d_model):
    seg, off = {}, 0
    widths = [(name, d_model if w is None else w) for name, w in _SEG]
    for name, w in sorted(widths, key=lambda nw: -nw[1]):
        assert off % w == 0, (name, off, w)
        seg[name] = (off, w)
        off += w
    total = -(-off // 512) * 512
    return seg, total


def _params(sem, vmem_mb=48, flags=None):
    return pltpu.CompilerParams(dimension_semantics=sem, vmem_limit_bytes=vmem_mb << 20, flags=flags)


def _tile(n, pref):
    t = min(n, pref)
    while n % t:
        t //= 2
    return t


def _mod_kernel(c_ref, w_ref, b_ref, o_ref):
    c = c_ref[...]
    a = (c * jax.nn.sigmoid(c)).astype(MXU)
    o_ref[...] = jnp.dot(a, w_ref[...].astype(MXU), preferred_element_type=F32) + b_ref[...]


def _modulation(cvec, w_mod, b_mod):
    L, D, N6 = w_mod.shape
    R = cvec.shape[0]
    tn = _tile(N6, 1024)
    return pl.pallas_call(
        _mod_kernel,
        name="mod",
        grid=(L, N6 // tn),
        in_specs=[pl.BlockSpec((R, D), lambda l, j: (0, 0)),
                  pl.BlockSpec((None, D, tn), lambda l, j: (l, 0, j)),
                  pl.BlockSpec((None, 1, tn), lambda l, j: (l, 0, j))],
        out_specs=pl.BlockSpec((None, R, tn), lambda l, j: (l, 0, j)),
        out_shape=jax.ShapeDtypeStruct((L, R, N6), F32),
        compiler_params=_params(("parallel", "parallel")),
    )(cvec, w_mod, b_mod.reshape(L, 1, N6))


def _inproj_kernel(x_ref, sh_ref, sc_ref, w_ref, o_ref, h_scr):
    @pl.when(pl.program_id(1) == 0)
    def _():
        h_scr[...] = (x_ref[...] * (1.0 + sc_ref[...]) + sh_ref[...]).astype(h_scr.dtype)

    o_ref[...] = jnp.dot(h_scr[...], w_ref[...], preferred_element_type=F32)


def _inproj(xs, mod3, w, row_of_tile, tm):
    n, D = xs.shape
    NP = w.shape[1]
    tn = _tile(NP, 1024)
    return pl.pallas_call(
        _inproj_kernel,
        name="inproj",
        grid=(n // tm, NP // tn),
        in_specs=[pl.BlockSpec((tm, D), lambda i, j: (i, 0)),
                  pl.BlockSpec((None, 1, D), lambda i, j: (row_of_tile(i, tm), 0, 0)),
                  pl.BlockSpec((None, 1, D), lambda i, j: (row_of_tile(i, tm), 0, 1)),
                  pl.BlockSpec((D, tn), lambda i, j: (0, j))],
        out_specs=pl.BlockSpec((tm, tn), lambda i, j: (i, j)),
        out_shape=jax.ShapeDtypeStruct((n, NP), F32),
        scratch_shapes=[pltpu.VMEM((tm, D), MXU)],
        compiler_params=_params(("parallel", "arbitrary")),
    )(xs, mod3, mod3, w)


def _swap_pairs(x, nf):
    lane = lax.broadcasted_iota(jnp.int32, x.shape, 1)
    up = pltpu.roll(x, LANES - nf, 1)
    down = pltpu.roll(x, nf, 1)
    return jnp.where((lane & nf) == 0, up, down)


def _rope(x, cos, sin, nf):
    return x * cos + _swap_pairs(x, nf) * sin


def _rms(x, g):
    return x * lax.rsqrt(jnp.mean(x * x, axis=-1, keepdims=True) + RMS_EPS) * g


def _prep_kernel(cq_ref, ckv_ref, kr_ref, rq_ref, rk_ref, gq_ref, gk_ref, gv_ref,
                 qn_ref, kvn_ref, wq_ref, wk_ref, wv_ref, gqn_ref, gkn_ref,
                 ca_ref, sa_ref, cb_ref, sb_ref, cc_ref, sc_ref,
                 qa_ref, ka_ref, va_ref, qb_ref, kb_ref, qc_ref, kc_ref, vc_ref):
    cqn = _rms(cq_ref[...], qn_ref[...]).astype(MXU)
    q = jnp.dot(cqn, wq_ref[...], preferred_element_type=F32)
    kvn = _rms(ckv_ref[...], kvn_ref[...]).astype(MXU)
    k = jnp.dot(kvn, wk_ref[...], preferred_element_type=F32)
    v = jnp.dot(kvn, wv_ref[...], preferred_element_type=F32)
    for h in range(MLA_HEADS):
        sl = slice(h * LANES, (h + 1) * LANES)
        va_ref[sl, :] = v[:, sl].T.astype(va_ref.dtype)
    ca, sa = ca_ref[...], sa_ref[...]
    kr = _rope(kr_ref[...], ca, sa, MLA_ROPE // 4)
    for h in range(MLA_HEADS):
        sl = slice(h * LANES, (h + 1) * LANES)
        qa_ref[:, sl] = (_rope(q[:, sl], ca, sa, MLA_ROPE // 4) * (MLA_SCALE * LOG2E)).astype(qa_ref.dtype)
        ka_ref[:, sl] = (k[:, sl] + kr).astype(ka_ref.dtype)
    for half in range(RET_HEADS * RET_DK // LANES):
        sl = slice(half * LANES, (half + 1) * LANES)
        cb, sb = cb_ref[:, sl], sb_ref[:, sl]
        qb_ref[:, sl] = _rope(rq_ref[:, sl], cb, sb, RET_DK // 4)
        kb_ref[:, sl] = _rope(rk_ref[:, sl], cb, sb, RET_DK // 4) * RET_K_SCALE
    cc, sc = cc_ref[...], sc_ref[...]
    for h in range(GQA_HEADS):
        sl = slice(h * LANES, (h + 1) * LANES)
        qc_ref[:, sl] = (_rope(_rms(gq_ref[:, sl], gqn_ref[...]), cc, sc, GQA_DH // 4)
                         * (GQA_SCALE * LOG2E)).astype(qc_ref.dtype)
    for h in range(GQA_KV_HEADS):
        sl = slice(h * LANES, (h + 1) * LANES)
        kc_ref[:, sl] = _rope(_rms(gk_ref[:, sl], gkn_ref[...]), cc, sc, GQA_DH // 4).astype(kc_ref.dtype)
    for h in range(GQA_KV_HEADS):
        sl = slice(h * LANES, (h + 1) * LANES)
        vc_ref[sl, :] = gv_ref[:, sl].T.astype(vc_ref.dtype)


def _prep(P, seg, wts, tabs, tab_of_tile, tm):
    n = P.shape[0]

    def pcol(name):
        off, w = seg[name]
        return pl.BlockSpec((tm, w), lambda i, o=off // w: (i, o))

    def const(a):
        return pl.BlockSpec(a.shape, lambda i: (0,) * a.ndim)

    def tab(a):
        return pl.BlockSpec((tm, a.shape[1]), lambda i: (tab_of_tile(i), 0))

    def rows(w):
        return pl.BlockSpec((tm, w), lambda i: (i, 0))

    consts = [wts["qn"], wts["kvn"], wts["wq"], wts["wk"], wts["wv"], wts["gqn"], wts["gkn"]]
    hq, hk = GQA_HEADS * LANES, GQA_KV_HEADS * LANES
    ha = MLA_HEADS * LANES
    hb = RET_HEADS * RET_DK
    outs = [(ha, MXU), (ha, MXU), (ha, MXU), (hb, F32), (hb, F32), (hq, MXU), (hk, MXU), (hk, MXU)]
    transposed = (2, 7)
    return pl.pallas_call(
        _prep_kernel,
        name="prep",
        grid=(n // tm,),
        in_specs=[pcol(s) for s in ("cq", "ckv", "kr", "rq", "rk", "gq", "gk", "gv")]
        + [const(a) for a in consts] + [tab(a) for a in tabs],
        out_specs=[pl.BlockSpec((w, tm), lambda i: (0, i)) if k in transposed else rows(w)
                   for k, (w, _) in enumerate(outs)],
        out_shape=[jax.ShapeDtypeStruct((w, n) if k in transposed else (n, w), dt)
                   for k, (w, dt) in enumerate(outs)],
        compiler_params=_params(("parallel",)),
    )(*([P] * 8), *consts, *tabs)


def _attn_kernel(*refs, group, tk, has_lat):
    if has_lat:
        q_ref, kc_ref, vc_ref, kl_ref, vl_ref, o_ref, acc_scr, s0_scr, s1_scr = refs
    else:
        q_ref, kc_ref, vc_ref, o_ref, acc_scr, s0_scr, s1_scr = refs
    s_scr = (s0_scr, s1_scr)
    tq = q_ref.shape[0]
    if group > 1:
        q = jnp.concatenate([q_ref[:, g * LANES:(g + 1) * LANES] for g in range(group)], axis=0)
    else:
        q = q_ref[...]
    rows = q.shape[0]
    chunks = [(kc_ref, vc_ref, 0, kc_ref.shape[0])]
    if has_lat:
        chunks += [(kl_ref, vl_ref, c * tk, tk) for c in range(kl_ref.shape[0] // tk)]

    NP = 1

    def scores(c, i):
        k_ref, _, off, size = chunks[c]
        r0, r1 = i * (size // NP), (i + 1) * (size // NP)
        s_scr[c % 2][r0:r1, :] = lax.dot_general(k_ref[off + r0:off + r1, :], q, (((1,), (1,)), ((), ())),
                                                 preferred_element_type=F32)

    def softmax_values(c, j, m, l):
        _, vt_ref, off, size = chunks[c]
        cols = slice(j * (rows // NP), (j + 1) * (rows // NP))
        s = s_scr[c % 2][:size, cols]
        m_new = jnp.maximum(m, jnp.max(s, axis=0, keepdims=True))
        a = jnp.exp2(m - m_new)
        p = jnp.exp2(s - m_new)
        acc_scr[:, cols] = a * acc_scr[:, cols] + jnp.dot(vt_ref[:, off:off + size], p.astype(MXU),
                                                          preferred_element_type=F32)
        return m_new, a * l + jnp.sum(p, axis=0, keepdims=True)

    acc_scr[...] = jnp.zeros_like(acc_scr)
    ml = [(jnp.full((1, rows // NP), NEG, F32), jnp.zeros((1, rows // NP), F32)) for _ in range(NP)]
    for i in range(NP):
        scores(0, i)
    for c in range(len(chunks)):
        for j in range(NP):
            if c + 1 < len(chunks):
                scores(c + 1, j)
            ml[j] = softmax_values(c, j, *ml[j])
    l = jnp.concatenate([l for _, l in ml], axis=1)
    o = acc_scr[...] * (1.0 / l)
    for g in range(group):
        o_ref[:, g * LANES:(g + 1) * LANES] = o[:, g * tq:(g + 1) * tq].T.astype(o_ref.dtype)


def _attention(q, kc, vct, kl, vlt, batch, group, rows_q):
    n, hq = q.shape
    hk = hq // (group * LANES)
    ctx_len = kc.shape[0] // batch
    has_lat = kl is not None
    tq = _tile(rows_q, 512 // group)
    nq = rows_q // tq
    in_specs = [pl.BlockSpec((tq, group * LANES), lambda b, h, i: (b * nq + i, h)),
                pl.BlockSpec((ctx_len, LANES), lambda b, h, i: (b, h)),
                pl.BlockSpec((LANES, ctx_len), lambda b, h, i: (h, b))]
    args = [q, kc, vct]
    tk = 512
    if has_lat:
        seq = kl.shape[0] // batch
        tk = _tile(seq, 1024)
        in_specs += [pl.BlockSpec((seq, LANES), lambda b, h, i: (b, h)),
                     pl.BlockSpec((LANES, seq), lambda b, h, i: (h, b))]
        args += [kl, vlt]
    return pl.pallas_call(
        functools.partial(_attn_kernel, group=group, tk=tk, has_lat=has_lat),
        name=f"attn_g{group}_{'lat' if has_lat else 'ctx'}",
        grid=(batch, hk, nq),
        in_specs=in_specs,
        out_specs=pl.BlockSpec((tq, group * LANES), lambda b, h, i: (b * nq + i, h)),
        out_shape=jax.ShapeDtypeStruct((n, hq), MXU),
        scratch_shapes=[pltpu.VMEM((LANES, group * tq), F32)]
        + [pltpu.VMEM((max(tk, ctx_len), group * tq), F32)] * 2,
        compiler_params=_params(("parallel", "parallel", "arbitrary")),
    )(*args)


def _ret_kernel(qf_ref, kf_ref, vf_ref, gf_ref, qb_ref, kb_ref, vb_ref, gb_ref,
                dm_ref, xi_ref, zeta_ref, gc_ref, avg_ref, r0_ref,
                of_ref, ob_ref, rT_ref, r_scr, yf_scr, yb_scr):
    j = pl.program_id(1)
    nch = pl.num_programs(1)
    C = RET_CHUNK

    @pl.when(j == 0)
    def _():
        r_scr[...] = r0_ref[...]

    def direction(d, q_ref, k_ref, v_ref):
        q = q_ref[...]
        k = k_ref[...]
        vb = v_ref[...].astype(MXU)
        qb = q.astype(MXU)
        qx = (q * xi_ref[d]).astype(MXU)
        kT = k.T
        kTb = kT.astype(MXU)
        kzT = (kT * zeta_ref[d]).astype(MXU)
        ys = []
        for h in range(RET_HEADS):
            ks = slice(h * RET_DK, (h + 1) * RET_DK)
            vs = slice(h * RET_DV, (h + 1) * RET_DV)
            r_h = r_scr[d, ks, :]
            s = jnp.dot(qb[:, ks], kTb[ks, :], preferred_element_type=F32) * dm_ref[d, h]
            y = (jnp.dot(s.astype(MXU), vb[:, vs], preferred_element_type=F32)
                 + jnp.dot(qx[:, ks], r_h.astype(MXU), preferred_element_type=F32))
            r_scr[d, ks, :] = gc_ref[d, ks, :] * r_h + jnp.dot(kzT[ks, :], vb[:, vs],
                                                               preferred_element_type=F32)
            ys.append(y)
        return ys

    ys_f = direction(0, qf_ref, kf_ref, vf_ref)
    ys_b = direction(1, qb_ref, kb_ref, vb_ref)
    row_f = pl.multiple_of(j * C, C)
    row_b = pl.multiple_of((nch - 1 - j) * C, C)

    y_f = jnp.concatenate(ys_f, axis=1)
    y_b = jnp.concatenate(ys_b, axis=1)

    @pl.when(2 * j < nch - 1)
    def _():
        yf_scr[pl.ds(row_f, C), :] = y_f
        yb_scr[pl.ds(row_b, C), :] = y_b

    def group_mean(a):
        hi = a.astype(MXU)
        lo = (a - hi.astype(F32)).astype(MXU)
        return (jnp.dot(hi, avg_ref[...], preferred_element_type=F32)
                + jnp.dot(lo, avg_ref[...], preferred_element_type=F32))

    def finish(y_here, other, row, g_ref, o_ref):
        y = y_here + other[pl.ds(row, C), :]
        yc = y - group_mean(y)
        yn = yc * lax.rsqrt(group_mean(yc * yc) + NORM_EPS)
        g = g_ref[...]
        o_ref[...] = (g * jax.nn.sigmoid(g) * yn).astype(o_ref.dtype)

    @pl.when(2 * j >= nch)
    def _():
        finish(y_f, yb_scr, row_f, gf_ref, of_ref)
        finish(y_b, yf_scr, row_b, gb_ref, ob_ref)

    @pl.when(j == nch - 1)
    def _():
        rT_ref[...] = r_scr[...]


def _retention(qb, kb, P, seg, dec, r0, batch):
    n = qb.shape[0]
    C = RET_CHUNK
    nch = n // batch // C
    assert nch % 2 == 0
    hk, hv = RET_HEADS * RET_DK, RET_HEADS * RET_DV
    cv, cg = seg["rv"][0] // hv, seg["rg"][0] // hv
    half = nch // 2

    fwd = lambda b, j: b * nch + j
    bwd = lambda b, j: b * nch + nch - 1 - j
    fwd_out = lambda b, j: b * nch + jnp.maximum(j, half)
    bwd_out = lambda b, j: b * nch + nch - 1 - jnp.maximum(j, half)
    whole = lambda a: pl.BlockSpec(a.shape, lambda b, j: (0,) * a.ndim)
    head_of = np.arange(hv) // RET_DV
    avg = jnp.asarray((head_of[:, None] == head_of[None, :]).astype(np.float32) / RET_DV, MXU)
    tabs = [dec["dm"], dec["xi"], dec["zetaT"], dec["gc"], avg]
    o_f, o_b, r_t = pl.pallas_call(
        _ret_kernel,
        name="retention",
        grid=(batch, nch),
        in_specs=[pl.BlockSpec((C, hk), lambda b, j: (fwd(b, j), 0)),
                  pl.BlockSpec((C, hk), lambda b, j: (fwd(b, j), 0)),
                  pl.BlockSpec((C, hv), lambda b, j: (fwd(b, j), cv)),
                  pl.BlockSpec((C, hv), lambda b, j: (fwd_out(b, j), cg)),
                  pl.BlockSpec((C, hk), lambda b, j: (bwd(b, j), 0)),
                  pl.BlockSpec((C, hk), lambda b, j: (bwd(b, j), 0)),
                  pl.BlockSpec((C, hv), lambda b, j: (bwd(b, j), cv)),
                  pl.BlockSpec((C, hv), lambda b, j: (bwd_out(b, j), cg))]
        + [whole(t) for t in tabs]
        + [pl.BlockSpec((None, 2, hk, RET_DV), lambda b, j: (b, 0, 0, 0))],
        out_specs=[pl.BlockSpec((C, hv), lambda b, j: (fwd_out(b, j), 0)),
                   pl.BlockSpec((C, hv), lambda b, j: (bwd_out(b, j), 0)),
                   pl.BlockSpec((None, 2, hk, RET_DV), lambda b, j: (b, 0, 0, 0))],
        out_shape=[jax.ShapeDtypeStruct((n, hv), MXU), jax.ShapeDtypeStruct((n, hv), MXU),
                   jax.ShapeDtypeStruct((batch, 2, hk, RET_DV), F32)],
        scratch_shapes=[pltpu.VMEM((2, hk, RET_DV), F32), pltpu.VMEM((nch * C, hv), F32),
                        pltpu.VMEM((nch * C, hv), F32)],
        compiler_params=_params(("parallel", "arbitrary")),
    )(qb, kb, P, P, qb, kb, P, P, *tabs, r0)
    o_f = o_f.reshape(batch, 2, half * C, hv)
    o_b = o_b.reshape(batch, 2, half * C, hv)
    return jnp.stack([o_b[:, 0], o_f[:, 1]], axis=1).reshape(n, hv), r_t


def _decay_tables(logit):
    C = RET_CHUNK
    log_g = jax.nn.log_sigmoid(logit.astype(F32))
    i = jnp.arange(C, dtype=F32)
    diff = i[:, None] - i[None, :]
    lg = log_g[:, :, None, None]
    fwd = jnp.where(diff >= 0, jnp.exp(lg[0] * jnp.maximum(diff, 0.0)), 0.0)
    bwd = jnp.where(diff <= 0, jnp.exp(lg[1] * jnp.maximum(-diff, 0.0)), 0.0)
    rep = lambda a: jnp.repeat(a, RET_DK, axis=0)
    xi_f = jnp.exp(log_g[0][:, None] * (i + 1.0))
    xi_b = jnp.exp(log_g[1][:, None] * (C - i))
    ze_f = jnp.exp(log_g[0][:, None] * (C - 1.0 - i))
    ze_b = jnp.exp(log_g[1][:, None] * i)
    gc = jnp.exp(log_g * C)
    return {
        "dm": jnp.stack([fwd, bwd]),
        "xi": jnp.stack([rep(xi_f).T, rep(xi_b).T]),
        "zetaT": jnp.stack([rep(ze_f), rep(ze_b)]),
        "gc": jnp.broadcast_to(jnp.repeat(gc, RET_DK, axis=1)[:, :, None],
                               (2, RET_HEADS * RET_DK, RET_DV)),
    }


def _merge_kernel(oa_ref, ob_ref, oc_ref, ga_ref, gb_ref, gc_ref, wa_ref, wb_ref, wc_ref, m_ref):
    m = (jax.nn.sigmoid(ga_ref[...]) * jnp.dot(oa_ref[...], wa_ref[...], preferred_element_type=F32)
         + jax.nn.sigmoid(gb_ref[...]) * jnp.dot(ob_ref[...], wb_ref[...], preferred_element_type=F32)
         + jax.nn.sigmoid(gc_ref[...]) * jnp.dot(oc_ref[...], wc_ref[...], preferred_element_type=F32))
    m_ref[...] = m.astype(m_ref.dtype)


def _merge(oa, ob, oc, P, seg, wa, wb, wc, tm):
    n = oa.shape[0]
    D = wa.shape[1]
    tn = _tile(D, 512)
    nb = D // tn

    def rows(a):
        return pl.BlockSpec((tm, a.shape[1]), lambda i, j: (i, 0))

    def gate(name):
        return pl.BlockSpec((tm, tn), lambda i, j, o=seg[name][0] // tn: (i, o + j))

    def wcol(a):
        return pl.BlockSpec((a.shape[0], tn), lambda i, j: (0, j))

    return pl.pallas_call(
        _merge_kernel,
        name="merge",
        grid=(n // tm, nb),
        in_specs=[rows(oa), rows(ob), rows(oc), gate("ga"), gate("gb"), gate("gc"),
                  wcol(wa), wcol(wb), wcol(wc)],
        out_specs=pl.BlockSpec((tm, tn), lambda i, j: (i, j)),
        out_shape=jax.ShapeDtypeStruct((n, D), MXU),
        compiler_params=_params(("parallel", "arbitrary")),
    )(oa, ob, oc, P, P, P, wa, wb, wc)


def _layer_norm(y, g, b):
    mu = jnp.mean(y, axis=-1, keepdims=True)
    yc = y - mu
    return yc * lax.rsqrt(jnp.mean(yc * yc, axis=-1, keepdims=True) + NORM_EPS) * g + b


def _outln_kernel(m_ref, w_ref, x_ref, g1_ref, lg_ref, lb_ref, o_ref, *, alpha):
    o = jnp.dot(m_ref[...], w_ref[...], preferred_element_type=F32)
    o_ref[...] = _layer_norm(alpha * x_ref[...] + g1_ref[...] * o, lg_ref[...], lb_ref[...])


def _outln(m, w, xs, mod3, lg, lb, row_of_tile, alpha, tm):
    n, D = xs.shape
    vec = pl.BlockSpec((1, D), lambda i: (0, 0))
    return pl.pallas_call(
        functools.partial(_outln_kernel, alpha=alpha),
        name="outln",
        grid=(n // tm,),
        in_specs=[pl.BlockSpec((tm, D), lambda i: (i, 0)),
                  pl.BlockSpec((D, D), lambda i: (0, 0)),
                  pl.BlockSpec((tm, D), lambda i: (i, 0)),
                  pl.BlockSpec((None, 1, D), lambda i: (row_of_tile(i, tm), 0, 2)),
                  vec, vec],
        out_specs=pl.BlockSpec((tm, D), lambda i: (i, 0)),
        out_shape=jax.ShapeDtypeStruct((n, D), F32),
        compiler_params=_params(("parallel",)),
    )(m, w, xs, mod3, lg, lb)


def _bitonic_merge(xs):
    n = len(xs)
    if n == 1:
        return xs
    half = n // 2
    hi = [jnp.maximum(xs[i], xs[i + half]) for i in range(half)]
    lo = [jnp.minimum(xs[i], xs[i + half]) for i in range(half)]
    return _bitonic_merge(hi) + _bitonic_merge(lo)


def _sort_desc(xs):
    n = len(xs)
    if n == 1:
        return xs
    return _bitonic_merge(_sort_desc(xs[:n // 2]) + _sort_desc(xs[n // 2:])[::-1])


def _top16_of(groups):
    while len(groups) > 1:
        nxt = []
        for a, b in zip(groups[0::2], groups[1::2]):
            nxt.append(_bitonic_merge([jnp.maximum(a[i], b[PEER_TOPK - 1 - i]) for i in range(PEER_TOPK)]))
        groups = nxt
    return groups[0]


def _top16(xs):
    return _top16_of([_sort_desc(xs[g:g + PEER_TOPK]) for g in range(0, len(xs), PEER_TOPK)])


def _router_kernel(x_ref, sh_ref, sc_ref, wq_ref, k1_ref, k2_ref, k2h_ref,
                   ht_ref, thr_ref, e1_ref, s2_ref, e2_ref, s1_scr, s2_scr):
    K, H = PEER_N_KEYS, PEER_HEADS
    tm = x_ref.shape[0]
    h = x_ref[...] * (1.0 + sc_ref[...]) + sh_ref[...]
    hT = h.T.astype(MXU)
    ht_ref[...] = hT
    qT = jnp.dot(wq_ref[...], hT, preferred_element_type=F32).astype(MXU)
    s1_scr[...] = jnp.dot(k1_ref[...], qT, preferred_element_type=F32).reshape(K, H, tm)
    s2_scr[...] = jnp.dot(k2_ref[...], qT, preferred_element_type=F32).reshape(K, H, tm)
    s2_ref[...] = jnp.dot(k2h_ref[...], qT, preferred_element_type=F32).reshape(H, K, tm)

    def lane_block(c, _):
        ls = pl.ds(pl.multiple_of(c * LANES, LANES), LANES)
        s1 = s1_scr[:, :, ls]
        v1 = _top16([s1[k] for k in range(K)])
        v2 = _top16([s2_scr[k, :, ls] for k in range(K)])
        pairs = [(a, b) for a in range(PEER_TOPK) for b in range(PEER_TOPK)
                 if (a + 1) * (b + 1) <= PEER_TOPK]
        cand = [v1[a] + v2[b] for a, b in pairs]
        pad = [jnp.full_like(cand[0], -jnp.inf)] * (-len(cand) % PEER_TOPK)
        t = _top16(cand + pad)[-1]
        ex1 = [jnp.exp(v - v1[0]) for v in v1]
        ex2 = [jnp.exp(v - v2[0]) for v in v2]
        z = jnp.zeros_like(t)
        for (a, b), cv in zip(pairs, cand):
            z = z + jnp.where(cv >= t, ex1[a] * ex2[b], 0.0)
        inv_z = 1.0 / z
        thr = jnp.full(s1.shape, jnp.inf, F32)
        for b in range(PEER_TOPK):
            thr = jnp.where(s1 + v2[b][None] >= t[None], v2[b][None], thr)
        thr_ref[:, :, ls] = thr
        e1_ref[:, :, ls] = jnp.where(s1 >= v1[-1][None], jnp.exp(s1 - v1[0][None]), 0.0)
        for hh in range(H):
            s2h = s2_ref[hh, :, ls]
            e2_ref[hh, :, ls] = jnp.where(
                s2h >= v2[-1][hh:hh + 1], jnp.exp(s2h - v2[0][hh:hh + 1]) * inv_z[hh:hh + 1], 0.0)
        return 0

    lax.fori_loop(0, tm // LANES, lane_block, 0)


def _router(x1, mod3, wts, row_of_tile, tm):
    n, D = x1.shape
    K, H = PEER_N_KEYS, PEER_HEADS
    HQ = H * PEER_DQ

    def const(a):
        return pl.BlockSpec(a.shape, lambda i: (0,) * a.ndim)

    kh = pl.BlockSpec((K, H, tm), lambda i: (0, 0, i))
    hk = pl.BlockSpec((H, K, tm), lambda i: (0, 0, i))
    return pl.pallas_call(
        _router_kernel,
        name="router",
        grid=(n // tm,),
        in_specs=[pl.BlockSpec((tm, D), lambda i: (i, 0)),
                  pl.BlockSpec((None, 1, D), lambda i: (row_of_tile(i, tm), 0, 3)),
                  pl.BlockSpec((None, 1, D), lambda i: (row_of_tile(i, tm), 0, 4)),
                  const(wts["wqT"]), const(wts["k1kh"]), const(wts["k2kh"]), const(wts["k2hk"])],
        out_specs=[pl.BlockSpec((D, tm), lambda i: (0, i)), kh, kh, hk, hk],
        out_shape=[jax.ShapeDtypeStruct((D, n), MXU),
                   jax.ShapeDtypeStruct((K, H, n), F32), jax.ShapeDtypeStruct((K, H, n), F32),
                   jax.ShapeDtypeStruct((H, K, n), F32), jax.ShapeDtypeStruct((H, K, n), F32)],
        scratch_shapes=[pltpu.VMEM((K, H, tm), F32), pltpu.VMEM((K, H, tm), F32)],
        compiler_params=_params(("parallel",)),
    )(x1, mod3, mod3, wts["wqT"], wts["k1kh"], wts["k2kh"], wts["k2hk"])


PEER_ROWS = 32
PEER_TE = 512


def _peer_kernel(go_ref, ua_ref, ub_ref, u0_ref, vta_ref, vtb_ref, vtl_ref, ht_ref,
                 thr_ref, e1_ref, s2_ref, e2_ref, x_ref, g2_ref, lg_ref, lb_ref, o_ref,
                 acc, sc0, sc1, a0, a1, *, alpha):
    e = pl.program_id(1)
    last = pl.num_programs(1) - 1
    te, tm = sc0.shape
    K = PEER_N_KEYS
    nsub = te // K
    go1, go2 = go_ref[0] != 0, go_ref[1] != 0

    nval = nsub
    nsc = nsub
    dr = acc.shape[0] // nval

    def scores(u_tile_ref, sc):
        def piece(i):
            rows = slice(i * (te // nsc), (i + 1) * (te // nsc))
            sc[rows, :] = jnp.dot(u_tile_ref[rows, :], ht_ref[...], preferred_element_type=F32)
        return [functools.partial(piece, i) for i in range(nsc)]

    def values(vt_tile_ref, a):
        def piece(r):
            rows = slice(r * dr, (r + 1) * dr)
            acc[rows, :] += jnp.dot(vt_tile_ref[rows, :], a[...], preferred_element_type=F32)
        return [functools.partial(piece, r) for r in range(nval)]

    def gates(sc, a, tile):
        def piece(c, rb):
            ls = slice(c * LANES, (c + 1) * LANES)
            rs = slice(rb * PEER_ROWS, (rb + 1) * PEER_ROWS)
            thr = [thr_ref[tile * nsub + sub, :, ls] for sub in range(nsub)]
            e1r = [e1_ref[tile * nsub + sub, :, ls] for sub in range(nsub)]
            g = [None] * nsub
            for h in range(PEER_HEADS):
                s2p, e2p = s2_ref[h, rs, ls], e2_ref[h, rs, ls]
                for sub in range(nsub):
                    w = jnp.where(s2p >= thr[sub][h:h + 1], e2p, 0.0) * e1r[sub][h:h + 1]
                    g[sub] = w if g[sub] is None else g[sub] + w
            for sub in range(nsub):
                xs = slice(sub * K + rb * PEER_ROWS, sub * K + (rb + 1) * PEER_ROWS)
                x = sc[xs, ls]
                act = 0.5 * x * (1.0 + lax.erf(x * SQRT_HALF))
                a[xs, ls] = (act * g[sub]).astype(a.dtype)
        return [functools.partial(piece, c, rb) for c in range(tm // LANES) for rb in range(K // PEER_ROWS)]

    def interleave(matmul_pieces, vector_pieces):
        per = -(-len(vector_pieces) // len(matmul_pieces))
        for i, mm in enumerate(matmul_pieces):
            mm()
            for vp in vector_pieces[i * per:(i + 1) * per]:
                vp()

    def mix(xs, ys):
        if len(xs) < len(ys):
            xs, ys = ys, xs
        step = len(xs) // len(ys)
        out = []
        for i, x in enumerate(xs):
            out.append(x)
            if i % step == step - 1 and i // step < len(ys):
                out.append(ys[i // step])
        return out

    @pl.when(e == 0)
    def _():
        acc[...] = jnp.zeros_like(acc)
        a1[...] = jnp.zeros_like(a1)
        for p in scores(u0_ref, sc0):
            p()

    @pl.when(go1)
    def _():
        interleave(mix(scores(ub_ref, sc1), values(vtb_ref, a1)), gates(sc0, a0, 2 * e))

    @pl.when(go2)
    def _():
        interleave(mix(scores(ua_ref, sc0), values(vta_ref, a0)), gates(sc1, a1, 2 * e + 1))

    @pl.when(e == last)
    def _():
        for p in values(vtl_ref, a1):
            p()
        y = alpha * x_ref[...] + g2_ref[...] * acc[...].T
        o_ref[...] = _layer_norm(y, lg_ref[...], lb_ref[...])


def _peer(x1, routed, u, vt, mod3, lg, lb, row_of_tile, alpha, tm):
    n, D = x1.shape
    K, H = PEER_N_KEYS, PEER_HEADS
    te = PEER_TE
    nt = u.shape[0] // te
    ht, thr, e1, s2, e2 = routed
    once = dict(pipeline_mode=pl.Buffered(1))
    kh = pl.BlockSpec((K, H, tm), lambda i, e: (0, 0, i), **once)
    hk = pl.BlockSpec((H, K, tm), lambda i, e: (0, 0, i), **once)
    vec = pl.BlockSpec((1, D), lambda i, e: (0, 0))
    return pl.pallas_call(
        functools.partial(_peer_kernel, alpha=alpha),
        name="peer",
        grid=(n // tm, nt // 2),
        in_specs=[pl.BlockSpec(memory_space=pltpu.SMEM),
                  pl.BlockSpec((te, D), lambda i, e: (jnp.minimum(2 * e + 2, nt - 1), 0)),
                  pl.BlockSpec((te, D), lambda i, e: (2 * e + 1, 0)),
                  pl.BlockSpec((te, D), lambda i, e: (0, 0), **once),
                  pl.BlockSpec((D, te), lambda i, e: (0, 2 * e)),
                  pl.BlockSpec((D, te), lambda i, e: (0, jnp.maximum(2 * e - 1, 0))),
                  pl.BlockSpec((D, te), lambda i, e: (0, nt - 1), **once),
                  pl.BlockSpec((D, tm), lambda i, e: (0, i), **once),
                  kh, kh, hk, hk,
                  pl.BlockSpec((tm, D), lambda i, e: (i, 0), **once),
                  pl.BlockSpec((None, 1, D), lambda i, e: (row_of_tile(i, tm), 0, 5)),
                  vec, vec],
        out_specs=pl.BlockSpec((tm, D), lambda i, e: (i, 0)),
        out_shape=jax.ShapeDtypeStruct((n, D), F32),
        scratch_shapes=[pltpu.VMEM((D, tm), F32), pltpu.VMEM((te, tm), F32), pltpu.VMEM((te, tm), F32),
                        pltpu.VMEM((te, tm), MXU), pltpu.VMEM((te, tm), MXU)],
        compiler_params=_params(("arbitrary", "arbitrary"), 58),
    )(jnp.ones((2,), jnp.int32), u, u, u, vt, vt, vt, ht, thr, e1, s2, e2, x1, mod3, lg, lb)


def _pad_heads(w, heads, width, lo=0):
    lead = w.shape[:-1]
    w = w.reshape(*lead, heads, width)
    w = jnp.pad(w, [(0, 0)] * len(lead) + [(0, 0), (lo, LANES - lo - width)])
    return w.reshape(*lead, heads * LANES)


def _layer_weights(l, seg, total, w_in, mla_q_norm, mla_w_qup, mla_kv_norm, mla_w_kvup, gqa_q_norm,
                   gqa_k_norm, w_br_a, w_br_b, w_br_c, w_out, peer_w_q, peer_k1, peer_k2, peer_u, peer_v):
    D = w_in.shape[1]
    widths = (MLA_Q_LORA, MLA_KV_LORA, MLA_ROPE, RET_HEADS * RET_DK, RET_HEADS * RET_DK,
              RET_HEADS * RET_DV, RET_HEADS * RET_DV, GQA_HEADS * GQA_DH, GQA_KV_HEADS * GQA_DH,
              GQA_KV_HEADS * GQA_DH, D, D, D)
    names = ("cq", "ckv", "kr", "rq", "rk", "rv", "rg", "gq", "gk", "gv", "ga", "gb", "gc")
    cols = dict(zip(names, jnp.split(w_in[l], np.cumsum(widths)[:-1].tolist(), axis=1)))
    lo = MLA_NOPE
    cols["kr"] = jnp.pad(cols["kr"], ((0, 0), (lo, LANES - lo - MLA_ROPE)))
    order = sorted(seg, key=lambda name: seg[name][0])
    used = seg[order[-1]][0] + seg[order[-1]][1]
    win = jnp.concatenate([cols[name] for name in order] + [jnp.zeros((D, total - used), F32)],
                          axis=1).astype(MXU)
    kv = mla_w_kvup[l].reshape(MLA_KV_LORA, MLA_HEADS, MLA_NOPE + MLA_V)
    eye = jnp.eye(PEER_HEADS, dtype=F32)
    half = PEER_DQ // 2
    k1p = jnp.pad(peer_k1[l], ((0, 0), (0, half)))
    k2p = jnp.pad(peer_k2[l], ((0, 0), (half, 0)))
    HQ = PEER_HEADS * PEER_DQ
    return {
        "win": win,
        "qn": mla_q_norm[l][None], "kvn": mla_kv_norm[l][None],
        "wq": _pad_heads(mla_w_qup[l], MLA_HEADS, MLA_NOPE + MLA_ROPE).astype(MXU),
        "wk": _pad_heads(kv[..., :MLA_NOPE].reshape(MLA_KV_LORA, -1), MLA_HEADS, MLA_NOPE).astype(MXU),
        "wv": _pad_heads(kv[..., MLA_NOPE:].reshape(MLA_KV_LORA, -1), MLA_HEADS, MLA_V).astype(MXU),
        "gqn": gqa_q_norm[l][None], "gkn": gqa_k_norm[l][None],
        "wa": _pad_heads(w_br_a[l].T, MLA_HEADS, MLA_V).T.astype(MXU),
        "wb": w_br_b[l].astype(MXU), "wc": w_br_c[l].astype(MXU), "wo": w_out[l].astype(MXU),
        "wqT": peer_w_q[l].T.astype(MXU),
        "k1kh": jnp.einsum("kd,hg->khgd", k1p, eye).reshape(HQ, HQ).astype(MXU),
        "k2kh": jnp.einsum("kd,hg->khgd", k2p, eye).reshape(HQ, HQ).astype(MXU),
        "k2hk": jnp.einsum("kd,hg->hkgd", k2p, eye).reshape(HQ, HQ).astype(MXU),
        "u": peer_u[l].astype(MXU), "vt": peer_v[l].T.astype(MXU),
    }


def _rope_tables(seq):
    t = jnp.arange(seq, dtype=jnp.int32)
    row, col = (t // GRID_W).astype(F32), (t % GRID_W).astype(F32)

    def tab(r):
        nf = r // 4
        inv = ROPE_BASE ** (-jnp.arange(nf, dtype=F32) / nf)
        ar, ac = row[:, None] * inv[None], col[:, None] * inv[None]
        cos = jnp.concatenate([jnp.cos(ar)] * 2 + [jnp.cos(ac)] * 2, axis=1)
        sin = jnp.concatenate([-jnp.sin(ar), jnp.sin(ar), -jnp.sin(ac), jnp.sin(ac)], axis=1)
        return cos, sin

    ca, sa = tab(MLA_ROPE)
    lo, hi = MLA_NOPE, LANES - MLA_NOPE - MLA_ROPE
    ca = jnp.pad(ca, ((0, 0), (lo, hi)), constant_values=1.0)
    sa = jnp.pad(sa, ((0, 0), (lo, hi)))
    cb, sb = tab(RET_DK)
    cb, sb = jnp.tile(cb, (1, RET_HEADS)), jnp.tile(sb, (1, RET_HEADS))
    cc, sc = tab(GQA_DH)
    return [ca, sa, cb, sb, cc, sc]


def kernel(x, c, ctx, c_ctx, w_mod, b_mod, w_in, mla_q_norm, mla_w_qup, mla_kv_norm, mla_w_kvup, ret_decay_logit, gqa_q_norm, gqa_k_norm, w_br_a, w_br_b, w_br_c, w_out, ln1_g, ln1_b, peer_w_q, peer_k1, peer_k2, peer_u, peer_v, ln2_g, ln2_b):
    B, S, D = x.shape
    CTX = ctx.shape[1]
    L = w_in.shape[0]
    alpha = float((2.0 * L) ** 0.25)
    seg, total = _layout(D)

    tm_l = _tile(S, 512)
    tm_c = _tile(B * CTX, 512)
    tpb = S // tm_l
    lat_row = lambda i, tm: (i * tm) // S
    ctx_row = lambda i, tm: B
    lat_tab = lambda i: i % tpb
    ctx_tab = lambda i: 0

    nrow = -(-(B + 1) // 8) * 8
    cvec = jnp.concatenate([c, c_ctx[None], jnp.zeros((nrow - B - 1, D), F32)], axis=0)
    mod = _modulation(cvec, w_mod, b_mod)

    tabs_l = _rope_tables(S)
    tabs_c = [jnp.ones((tm_c, a.shape[1]), F32) if k % 2 == 0 else jnp.zeros((tm_c, a.shape[1]), F32)
              for k, a in enumerate(tabs_l)]

    xl = x.reshape(B * S, D)
    xc = ctx.reshape(B * CTX, D)
    for l in range(L):
        last = l == L - 1
        W = _layer_weights(l, seg, total, w_in, mla_q_norm, mla_w_qup, mla_kv_norm, mla_w_kvup,
                           gqa_q_norm, gqa_k_norm, w_br_a, w_br_b, w_br_c, w_out,
                           peer_w_q, peer_k1, peer_k2, peer_u, peer_v)
        dec = _decay_tables(ret_decay_logit[l])
        mod3 = mod[l].reshape(nrow, 1, 6 * D)
        lg1, lb1, lg2, lb2 = ln1_g[l][None], ln1_b[l][None], ln2_g[l][None], ln2_b[l][None]

        Pl = _inproj(xl, mod3, W["win"], lat_row, _tile(S, 1024))
        Pc = _inproj(xc, mod3, W["win"], ctx_row, tm_c)
        qa, ka, va, qb, kb, qc, kc, vc = _prep(Pl, seg, W, tabs_l, lat_tab, tm_l)
        qa_c, ka_c, va_c, qb_c, kb_c, qc_c, kc_c, vc_c = _prep(Pc, seg, W, tabs_c, ctx_tab, tm_c)

        oa = _attention(qa, ka_c, va_c, ka, va, B, 1, S)
        oc = _attention(qc, kc_c, vc_c, kc, vc, B, GQA_GROUP, S)
        r0 = jnp.zeros((B, 2, RET_HEADS * RET_DK, RET_DV), F32)
        ob_c, r_ctx = _retention(qb_c, kb_c, Pc, seg, dec, r0, B)
        ob, _ = _retention(qb, kb, Pl, seg, dec, r_ctx, B)

        def tail(xs, oa, ob, oc, P, row_of_tile, tm):
            m = _merge(oa, ob, oc, P, seg, W["wa"], W["wb"], W["wc"], _tile(oa.shape[0], 1024))
            x1 = _outln(m, W["wo"], xs, mod3, lg1, lb1, row_of_tile, alpha, _tile(tm, 256))
            routed = _router(x1, mod3, W, row_of_tile, tm)
            return _peer(x1, routed, W["u"], W["vt"], mod3, lg2, lb2, row_of_tile, alpha, tm)

        if not last:
            oa_c = _attention(qa_c, ka_c, va_c, None, None, B, 1, CTX)
            oc_c = _attention(qc_c, kc_c, vc_c, None, None, B, GQA_GROUP, CTX)
            xc = tail(xc, oa_c, ob_c, oc_c, Pc, ctx_row, tm_c)
        xl = tail(xl, oa, ob, oc, Pl, lat_row, tm_l)
    return xl.reshape(B, S, D)
```

```python
import functools

import numpy as np
import jax
import jax.numpy as jnp
from jax import lax
from jax.experimental import pallas as pl
from jax.experimental.pallas import tpu as pltpu

GRID_W = 64
ROPE_BASE = 10000.0
NORM_EPS = 1e-5
RMS_EPS = 1e-6
MLA_HEADS, MLA_Q_LORA, MLA_KV_LORA, MLA_NOPE, MLA_ROPE, MLA_V = 8, 512, 256, 64, 32, 64
MLA_SCALE = (MLA_NOPE + MLA_ROPE) ** -0.5
RET_HEADS, RET_DK, RET_DV, RET_CHUNK = 8, 32, 64, 128
RET_K_SCALE = RET_DK ** -0.5
GQA_HEADS, GQA_KV_HEADS, GQA_DH = 8, 2, 128
GQA_GROUP = GQA_HEADS // GQA_KV_HEADS
GQA_SCALE = GQA_DH ** -0.5
PEER_HEADS, PEER_DQ, PEER_N_KEYS, PEER_TOPK = 8, 128, 128, 16
PEER_N_EXPERTS = PEER_N_KEYS * PEER_N_KEYS

LANES = 128
MXU = jnp.bfloat16
F32 = jnp.float32
NEG = -1e30
SQRT_HALF = float(np.sqrt(0.5))
LOG2E = float(np.log2(np.e))

_SEG = (("ga", None), ("gb", None), ("gc", None), ("gq", GQA_HEADS * GQA_DH),
        ("cq", MLA_Q_LORA), ("rv", RET_HEADS * RET_DV), ("rg", RET_HEADS * RET_DV),
        ("ckv", MLA_KV_LORA), ("rq", RET_HEADS * RET_DK), ("rk", RET_HEADS * RET_DK),
        ("gk", GQA_KV_HEADS * GQA_DH), ("gv", GQA_KV_HEADS * GQA_DH), ("kr", LANES))


def _layout(d_model):
    seg, off = {}, 0
    widths = [(name, d_model if w is None else w) for name, w in _SEG]
    for name, w in sorted(widths, key=lambda nw: -nw[1]):
        assert off % w == 0, (name, off, w)
        seg[name] = (off, w)
        off += w
    total = -(-off // 512) * 512
    return seg, total


def _params(sem, vmem_mb=48, flags=None):
    return pltpu.CompilerParams(dimension_semantics=sem, vmem_limit_bytes=vmem_mb << 20, flags=flags)


def _tile(n, pref):
    t = min(n, pref)
    while n % t:
        t //= 2
    return t


def _mod_kernel(c_ref, w_ref, b_ref, o_ref):
    c = c_ref[...]
    a = (c * jax.nn.sigmoid(c)).astype(MXU)
    o_ref[...] = jnp.dot(a, w_ref[...].astype(MXU), preferred_element_type=F32) + b_ref[...]


def _modulation(cvec, w_mod, b_mod):
    L, D, N6 = w_mod.shape
    R = cvec.shape[0]
    tn = _tile(N6, 1024)
    return pl.pallas_call(
        _mod_kernel,
        name="mod",
        grid=(L, N6 // tn),
        in_specs=[pl.BlockSpec((R, D), lambda l, j: (0, 0)),
                  pl.BlockSpec((None, D, tn), lambda l, j: (l, 0, j)),
                  pl.BlockSpec((None, 1, tn), lambda l, j: (l, 0, j))],
        out_specs=pl.BlockSpec((None, R, tn), lambda l, j: (l, 0, j)),
        out_shape=jax.ShapeDtypeStruct((L, R, N6), F32),
        compiler_params=_params(("parallel", "parallel")),
    )(cvec, w_mod, b_mod.reshape(L, 1, N6))


def _inproj_kernel(x_ref, sh_ref, sc_ref, w_ref, o_ref, h_scr):
    @pl.when(pl.program_id(1) == 0)
    def _():
        h_scr[...] = (x_ref[...] * (1.0 + sc_ref[...]) + sh_ref[...]).astype(h_scr.dtype)

    o_ref[...] = jnp.dot(h_scr[...], w_ref[...], preferred_element_type=F32)


def _inproj(xs, mod3, w, row_of_tile, tm):
    n, D = xs.shape
    NP = w.shape[1]
    tn = _tile(NP, 1024)
    return pl.pallas_call(
        _inproj_kernel,
        name="inproj",
        grid=(n // tm, NP // tn),
        in_specs=[pl.BlockSpec((tm, D), lambda i, j: (i, 0)),
                  pl.BlockSpec((None, 1, D), lambda i, j: (row_of_tile(i, tm), 0, 0)),
                  pl.BlockSpec((None, 1, D), lambda i, j: (row_of_tile(i, tm), 0, 1)),
                  pl.BlockSpec((D, tn), lambda i, j: (0, j))],
        out_specs=pl.BlockSpec((tm, tn), lambda i, j: (i, j)),
        out_shape=jax.ShapeDtypeStruct((n, NP), F32),
        scratch_shapes=[pltpu.VMEM((tm, D), MXU)],
        compiler_params=_params(("parallel", "arbitrary")),
    )(xs, mod3, mod3, w)


def _swap_pairs(x, nf):
    lane = lax.broadcasted_iota(jnp.int32, x.shape, 1)
    up = pltpu.roll(x, LANES - nf, 1)
    down = pltpu.roll(x, nf, 1)
    return jnp.where((lane & nf) == 0, up, down)


def _rope(x, cos, sin, nf):
    return x * cos + _swap_pairs(x, nf) * sin


def _rms(x, g):
    return x * lax.rsqrt(jnp.mean(x * x, axis=-1, keepdims=True) + RMS_EPS) * g


def _prep_kernel(cq_ref, ckv_ref, kr_ref, rq_ref, rk_ref, gq_ref, gk_ref, gv_ref,
                 qn_ref, kvn_ref, wq_ref, wk_ref, wv_ref, gqn_ref, gkn_ref,
                 ca_ref, sa_ref, cb_ref, sb_ref, cc_ref, sc_ref,
                 qa_ref, ka_ref, va_ref, qb_ref, kb_ref, qc_ref, kc_ref, vc_ref):
    cqn = _rms(cq_ref[...], qn_ref[...]).astype(MXU)
    q = jnp.dot(cqn, wq_ref[...], preferred_element_type=F32)
    kvn = _rms(ckv_ref[...], kvn_ref[...]).astype(MXU)
    k = jnp.dot(kvn, wk_ref[...], preferred_element_type=F32)
    v = jnp.dot(kvn, wv_ref[...], preferred_element_type=F32)
    for h in range(MLA_HEADS):
        sl = slice(h * LANES, (h + 1) * LANES)
        va_ref[sl, :] = v[:, sl].T.astype(va_ref.dtype)
    ca, sa = ca_ref[...], sa_ref[...]
    kr = _rope(kr_ref[...], ca, sa, MLA_ROPE // 4)
    for h in range(MLA_HEADS):
        sl = slice(h * LANES, (h + 1) * LANES)
        qa_ref[:, sl] = (_rope(q[:, sl], ca, sa, MLA_ROPE // 4) * (MLA_SCALE * LOG2E)).astype(qa_ref.dtype)
        ka_ref[:, sl] = (k[:, sl] + kr).astype(ka_ref.dtype)
    for half in range(RET_HEADS * RET_DK // LANES):
        sl = slice(half * LANES, (half + 1) * LANES)
        cb, sb = cb_ref[:, sl], sb_ref[:, sl]
        qb_ref[:, sl] = _rope(rq_ref[:, sl], cb, sb, RET_DK // 4)
        kb_ref[:, sl] = _rope(rk_ref[:, sl], cb, sb, RET_DK // 4) * RET_K_SCALE
    cc, sc = cc_ref[...], sc_ref[...]
    for h in range(GQA_HEADS):
        sl = slice(h * LANES, (h + 1) * LANES)
        qc_ref[:, sl] = (_rope(_rms(gq_ref[:, sl], gqn_ref[...]), cc, sc, GQA_DH // 4)
                         * (GQA_SCALE * LOG2E)).astype(qc_ref.dtype)
    for h in range(GQA_KV_HEADS):
        sl = slice(h * LANES, (h + 1) * LANES)
        kc_ref[:, sl] = _rope(_rms(gk_ref[:, sl], gkn_ref[...]), cc, sc, GQA_DH // 4).astype(kc_ref.dtype)
    for h in range(GQA_KV_HEADS):
        sl = slice(h * LANES, (h + 1) * LANES)
        vc_ref[sl, :] = gv_ref[:, sl].T.astype(vc_ref.dtype)


def _prep(P, seg, wts, tabs, tab_of_tile, tm):
    n = P.shape[0]

    def pcol(name):
        off, w = seg[name]
        return pl.BlockSpec((tm, w), lambda i, o=off // w: (i, o))

    def const(a):
        return pl.BlockSpec(a.shape, lambda i: (0,) * a.ndim)

    def tab(a):
        return pl.BlockSpec((tm, a.shape[1]), lambda i: (tab_of_tile(i), 0))

    def rows(w):
        return pl.BlockSpec((tm, w), lambda i: (i, 0))

    consts = [wts["qn"], wts["kvn"], wts["wq"], wts["wk"], wts["wv"], wts["gqn"], wts["gkn"]]
    hq, hk = GQA_HEADS * LANES, GQA_KV_HEADS * LANES
    ha = MLA_HEADS * LANES
    hb = RET_HEADS * RET_DK
    outs = [(ha, MXU), (ha, MXU), (ha, MXU), (hb, F32), (hb, F32), (hq, MXU), (hk, MXU), (hk, MXU)]
    transposed = (2, 7)
    return pl.pallas_call(
        _prep_kernel,
        name="prep",
        grid=(n // tm,),
        in_specs=[pcol(s) for s in ("cq", "ckv", "kr", "rq", "rk", "gq", "gk", "gv")]
        + [const(a) for a in consts] + [tab(a) for a in tabs],
        out_specs=[pl.BlockSpec((w, tm), lambda i: (0, i)) if k in transposed else rows(w)
                   for k, (w, _) in enumerate(outs)],
        out_shape=[jax.ShapeDtypeStruct((w, n) if k in transposed else (n, w), dt)
                   for k, (w, dt) in enumerate(outs)],
        compiler_params=_params(("parallel",)),
    )(*([P] * 8), *consts, *tabs)


def _attn_kernel(*refs, group, tk, has_lat):
    if has_lat:
        q_ref, kc_ref, vc_ref, kl_ref, vl_ref, o_ref, acc_scr, s0_scr, s1_scr = refs
    else:
        q_ref, kc_ref, vc_ref, o_ref, acc_scr, s0_scr, s1_scr = refs
    s_scr = (s0_scr, s1_scr)
    tq = q_ref.shape[0]
    if group > 1:
        q = jnp.concatenate([q_ref[:, g * LANES:(g + 1) * LANES] for g in range(group)], axis=0)
    else:
        q = q_ref[...]
    rows = q.shape[0]
    chunks = [(kc_ref, vc_ref, 0, kc_ref.shape[0])]
    if has_lat:
        chunks += [(kl_ref, vl_ref, c * tk, tk) for c in range(kl_ref.shape[0] // tk)]

    NP = 1

    def scores(c, i):
        k_ref, _, off, size = chunks[c]
        r0, r1 = i * (size // NP), (i + 1) * (size // NP)
        s_scr[c % 2][r0:r1, :] = lax.dot_general(k_ref[off + r0:off + r1, :], q, (((1,), (1,)), ((), ())),
                                                 preferred_element_type=F32)

    def softmax_values(c, j, m, l):
        _, vt_ref, off, size = chunks[c]
        cols = slice(j * (rows // NP), (j + 1) * (rows // NP))
        s = s_scr[c % 2][:size, cols]
        m_new = jnp.maximum(m, jnp.max(s, axis=0, keepdims=True))
        a = jnp.exp2(m - m_new)
        p = jnp.exp2(s - m_new)
        acc_scr[:, cols] = a * acc_scr[:, cols] + jnp.dot(vt_ref[:, off:off + size], p.astype(MXU),
                                                          preferred_element_type=F32)
        return m_new, a * l + jnp.sum(p, axis=0, keepdims=True)

    acc_scr[...] = jnp.zeros_like(acc_scr)
    ml = [(jnp.full((1, rows // NP), NEG, F32), jnp.zeros((1, rows // NP), F32)) for _ in range(NP)]
    for i in range(NP):
        scores(0, i)
    for c in range(len(chunks)):
        for j in range(NP):
            if c + 1 < len(chunks):
                scores(c + 1, j)
            ml[j] = softmax_values(c, j, *ml[j])
    l = jnp.concatenate([l for _, l in ml], axis=1)
    o = acc_scr[...] * (1.0 / l)
    for g in range(group):
        o_ref[:, g * LANES:(g + 1) * LANES] = o[:, g * tq:(g + 1) * tq].T.astype(o_ref.dtype)


def _attention(q, kc, vct, kl, vlt, batch, group, rows_q):
    n, hq = q.shape
    hk = hq // (group * LANES)
    ctx_len = kc.shape[0] // batch
    has_lat = kl is not None
    tq = _tile(rows_q, 512 // group)
    nq = rows_q // tq
    in_specs = [pl.BlockSpec((tq, group * LANES), lambda b, h, i: (b * nq + i, h)),
                pl.BlockSpec((ctx_len, LANES), lambda b, h, i: (b, h)),
                pl.BlockSpec((LANES, ctx_len), lambda b, h, i: (h, b))]
    args = [q, kc, vct]
    tk = 512
    if has_lat:
        seq = kl.shape[0] // batch
        tk = _tile(seq, 1024)
        in_specs += [pl.BlockSpec((seq, LANES), lambda b, h, i: (b, h)),
                     pl.BlockSpec((LANES, seq), lambda b, h, i: (h, b))]
        args += [kl, vlt]
    return pl.pallas_call(
        functools.partial(_attn_kernel, group=group, tk=tk, has_lat=has_lat),
        name=f"attn_g{group}_{'lat' if has_lat else 'ctx'}",
        grid=(batch, hk, nq),
        in_specs=in_specs,
        out_specs=pl.BlockSpec((tq, group * LANES), lambda b, h, i: (b * nq + i, h)),
        out_shape=jax.ShapeDtypeStruct((n, hq), MXU),
        scratch_shapes=[pltpu.VMEM((LANES, group * tq), F32)]
        + [pltpu.VMEM((max(tk, ctx_len), group * tq), F32)] * 2,
        compiler_params=_params(("parallel", "parallel", "arbitrary")),
    )(*args)


def _ret_kernel(qf_ref, kf_ref, vf_ref, gf_ref, qb_ref, kb_ref, vb_ref, gb_ref,
                dm_ref, xi_ref, zeta_ref, gc_ref, avg_ref, r0_ref,
                of_ref, ob_ref, rT_ref, r_scr, yf_scr, yb_scr):
    j = pl.program_id(1)
    nch = pl.num_programs(1)
    C = RET_CHUNK

    @pl.when(j == 0)
    def _():
        r_scr[...] = r0_ref[...]

    def direction(d, q_ref, k_ref, v_ref):
        q = q_ref[...]
        k = k_ref[...]
        vb = v_ref[...].astype(MXU)
        qb = q.astype(MXU)
        qx = (q * xi_ref[d]).astype(MXU)
        kT = k.T
        kTb = kT.astype(MXU)
        kzT = (kT * zeta_ref[d]).astype(MXU)
        ys = []
        for h in range(RET_HEADS):
            ks = slice(h * RET_DK, (h + 1) * RET_DK)
            vs = slice(h * RET_DV, (h + 1) * RET_DV)
            r_h = r_scr[d, ks, :]
            s = jnp.dot(qb[:, ks], kTb[ks, :], preferred_element_type=F32) * dm_ref[d, h]
            y = (jnp.dot(s.astype(MXU), vb[:, vs], preferred_element_type=F32)
                 + jnp.dot(qx[:, ks], r_h.astype(MXU), preferred_element_type=F32))
            r_scr[d, ks, :] = gc_ref[d, ks, :] * r_h + jnp.dot(kzT[ks, :], vb[:, vs],
                                                               preferred_element_type=F32)
            ys.append(y)
        return ys

    ys_f = direction(0, qf_ref, kf_ref, vf_ref)
    ys_b = direction(1, qb_ref, kb_ref, vb_ref)
    row_f = pl.multiple_of(j * C, C)
    row_b = pl.multiple_of((nch - 1 - j) * C, C)

    y_f = jnp.concatenate(ys_f, axis=1)
    y_b = jnp.concatenate(ys_b, axis=1)

    @pl.when(2 * j < nch - 1)
    def _():
        yf_scr[pl.ds(row_f, C), :] = y_f
        yb_scr[pl.ds(row_b, C), :] = y_b

    def group_mean(a):
        hi = a.astype(MXU)
        lo = (a - hi.astype(F32)).astype(MXU)
        return (jnp.dot(hi, avg_ref[...], preferred_element_type=F32)
                + jnp.dot(lo, avg_ref[...], preferred_element_type=F32))

    def finish(y_here, other, row, g_ref, o_ref):
        y = y_here + other[pl.ds(row, C), :]
        yc = y - group_mean(y)
        yn = yc * lax.rsqrt(group_mean(yc * yc) + NORM_EPS)
        g = g_ref[...]
        o_ref[...] = (g * jax.nn.sigmoid(g) * yn).astype(o_ref.dtype)

    @pl.when(2 * j >= nch)
    def _():
        finish(y_f, yb_scr, row_f, gf_ref, of_ref)
        finish(y_b, yf_scr, row_b, gb_ref, ob_ref)

    @pl.when(j == nch - 1)
    def _():
        rT_ref[...] = r_scr[...]


def _retention(qb, kb, P, seg, dec, r0, batch):
    n = qb.shape[0]
    C = RET_CHUNK
    nch = n // batch // C
    assert nch % 2 == 0
    hk, hv = RET_HEADS * RET_DK, RET_HEADS * RET_DV
    cv, cg = seg["rv"][0] // hv, seg["rg"][0] // hv
    half = nch // 2

    fwd = lambda b, j: b * nch + j
    bwd = lambda b, j: b * nch + nch - 1 - j
    fwd_out = lambda b, j: b * nch + jnp.maximum(j, half)
    bwd_out = lambda b, j: b * nch + nch - 1 - jnp.maximum(j, half)
    whole = lambda a: pl.BlockSpec(a.shape, lambda b, j: (0,) * a.ndim)
    head_of = np.arange(hv) // RET_DV
    avg = jnp.asarray((head_of[:, None] == head_of[None, :]).astype(np.float32) / RET_DV, MXU)
    tabs = [dec["dm"], dec["xi"], dec["zetaT"], dec["gc"], avg]
    o_f, o_b, r_t = pl.pallas_call(
        _ret_kernel,
        name="retention",
        grid=(batch, nch),
        in_specs=[pl.BlockSpec((C, hk), lambda b, j: (fwd(b, j), 0)),
                  pl.BlockSpec((C, hk), lambda b, j: (fwd(b, j), 0)),
                  pl.BlockSpec((C, hv), lambda b, j: (fwd(b, j), cv)),
                  pl.BlockSpec((C, hv), lambda b, j: (fwd_out(b, j), cg)),
                  pl.BlockSpec((C, hk), lambda b, j: (bwd(b, j), 0)),
                  pl.BlockSpec((C, hk), lambda b, j: (bwd(b, j), 0)),
                  pl.BlockSpec((C, hv), lambda b, j: (bwd(b, j), cv)),
                  pl.BlockSpec((C, hv), lambda b, j: (bwd_out(b, j), cg))]
        + [whole(t) for t in tabs]
        + [pl.BlockSpec((None, 2, hk, RET_DV), lambda b, j: (b, 0, 0, 0))],
        out_specs=[pl.BlockSpec((C, hv), lambda b, j: (fwd_out(b, j), 0)),
                   pl.BlockSpec((C, hv), lambda b, j: (bwd_out(b, j), 0)),
                   pl.BlockSpec((None, 2, hk, RET_DV), lambda b, j: (b, 0, 0, 0))],
        out_shape=[jax.ShapeDtypeStruct((n, hv), MXU), jax.ShapeDtypeStruct((n, hv), MXU),
                   jax.ShapeDtypeStruct((batch, 2, hk, RET_DV), F32)],
        scratch_shapes=[pltpu.VMEM((2, hk, RET_DV), F32), pltpu.VMEM((nch * C, hv), F32),
                        pltpu.VMEM((nch * C, hv), F32)],
        compiler_params=_params(("parallel", "arbitrary")),
    )(qb, kb, P, P, qb, kb, P, P, *tabs, r0)
    o_f = o_f.reshape(batch, 2, half * C, hv)
    o_b = o_b.reshape(batch, 2, half * C, hv)
    return jnp.stack([o_b[:, 0], o_f[:, 1]], axis=1).reshape(n, hv), r_t


def _decay_tables(logit):
    C = RET_CHUNK
    log_g = jax.nn.log_sigmoid(logit.astype(F32))
    i = jnp.arange(C, dtype=F32)
    diff = i[:, None] - i[None, :]
    lg = log_g[:, :, None, None]
    fwd = jnp.where(diff >= 0, jnp.exp(lg[0] * jnp.maximum(diff, 0.0)), 0.0)
    bwd = jnp.where(diff <= 0, jnp.exp(lg[1] * jnp.maximum(-diff, 0.0)), 0.0)
    rep = lambda a: jnp.repeat(a, RET_DK, axis=0)
    xi_f = jnp.exp(log_g[0][:, None] * (i + 1.0))
    xi_b = jnp.exp(log_g[1][:, None] * (C - i))
    ze_f = jnp.exp(log_g[0][:, None] * (C - 1.0 - i))
    ze_b = jnp.exp(log_g[1][:, None] * i)
    gc = jnp.exp(log_g * C)
    return {
        "dm": jnp.stack([fwd, bwd]),
        "xi": jnp.stack([rep(xi_f).T, rep(xi_b).T]),
        "zetaT": jnp.stack([rep(ze_f), rep(ze_b)]),
        "gc": jnp.broadcast_to(jnp.repeat(gc, RET_DK, axis=1)[:, :, None],
                               (2, RET_HEADS * RET_DK, RET_DV)),
    }


def _merge_kernel(oa_ref, ob_ref, oc_ref, ga_ref, gb_ref, gc_ref, wa_ref, wb_ref, wc_ref, m_ref):
    m = (jax.nn.sigmoid(ga_ref[...]) * jnp.dot(oa_ref[...], wa_ref[...], preferred_element_type=F32)
         + jax.nn.sigmoid(gb_ref[...]) * jnp.dot(ob_ref[...], wb_ref[...], preferred_element_type=F32)
         + jax.nn.sigmoid(gc_ref[...]) * jnp.dot(oc_ref[...], wc_ref[...], preferred_element_type=F32))
    m_ref[...] = m.astype(m_ref.dtype)


def _merge(oa, ob, oc, P, seg, wa, wb, wc, tm):
    n = oa.shape[0]
    D = wa.shape[1]
    tn = _tile(D, 512)
    nb = D // tn

    def rows(a):
        return pl.BlockSpec((tm, a.shape[1]), lambda i, j: (i, 0))

    def gate(name):
        return pl.BlockSpec((tm, tn), lambda i, j, o=seg[name][0] // tn: (i, o + j))

    def wcol(a):
        return pl.BlockSpec((a.shape[0], tn), lambda i, j: (0, j))

    return pl.pallas_call(
        _merge_kernel,
        name="merge",
        grid=(n // tm, nb),
        in_specs=[rows(oa), rows(ob), rows(oc), gate("ga"), gate("gb"), gate("gc"),
                  wcol(wa), wcol(wb), wcol(wc)],
        out_specs=pl.BlockSpec((tm, tn), lambda i, j: (i, j)),
        out_shape=jax.ShapeDtypeStruct((n, D), MXU),
        compiler_params=_params(("parallel", "arbitrary")),
    )(oa, ob, oc, P, P, P, wa, wb, wc)


def _layer_norm(y, g, b):
    mu = jnp.mean(y, axis=-1, keepdims=True)
    yc = y - mu
    return yc * lax.rsqrt(jnp.mean(yc * yc, axis=-1, keepdims=True) + NORM_EPS) * g + b


def _outln_kernel(m_ref, w_ref, x_ref, g1_ref, lg_ref, lb_ref, o_ref, *, alpha):
    o = jnp.dot(m_ref[...], w_ref[...], preferred_element_type=F32)
    o_ref[...] = _layer_norm(alpha * x_ref[...] + g1_ref[...] * o, lg_ref[...], lb_ref[...])


def _outln(m, w, xs, mod3, lg, lb, row_of_tile, alpha, tm):
    n, D = xs.shape
    vec = pl.BlockSpec((1, D), lambda i: (0, 0))
    return pl.pallas_call(
        functools.partial(_outln_kernel, alpha=alpha),
        name="outln",
        grid=(n // tm,),
        in_specs=[pl.BlockSpec((tm, D), lambda i: (i, 0)),
                  pl.BlockSpec((D, D), lambda i: (0, 0)),
                  pl.BlockSpec((tm, D), lambda i: (i, 0)),
                  pl.BlockSpec((None, 1, D), lambda i: (row_of_tile(i, tm), 0, 2)),
                  vec, vec],
        out_specs=pl.BlockSpec((tm, D), lambda i: (i, 0)),
        out_shape=jax.ShapeDtypeStruct((n, D), F32),
        compiler_params=_params(("parallel",)),
    )(m, w, xs, mod3, lg, lb)


def _bitonic_merge(xs):
    n = len(xs)
    if n == 1:
        return xs
    half = n // 2
    hi = [jnp.maximum(xs[i], xs[i + half]) for i in range(half)]
    lo = [jnp.minimum(xs[i], xs[i + half]) for i in range(half)]
    return _bitonic_merge(hi) + _bitonic_merge(lo)


def _sort_desc(xs):
    n = len(xs)
    if n == 1:
        return xs
    return _bitonic_merge(_sort_desc(xs[:n // 2]) + _sort_desc(xs[n // 2:])[::-1])


def _top16_of(groups):
    while len(groups) > 1:
        nxt = []
        for a, b in zip(groups[0::2], groups[1::2]):
            nxt.append(_bitonic_merge([jnp.maximum(a[i], b[PEER_TOPK - 1 - i]) for i in range(PEER_TOPK)]))
        groups = nxt
    return groups[0]


def _top16(xs):
    return _top16_of([_sort_desc(xs[g:g + PEER_TOPK]) for g in range(0, len(xs), PEER_TOPK)])


def _router_kernel(x_ref, sh_ref, sc_ref, wq_ref, k1_ref, k2_ref, k2h_ref,
                   ht_ref, thr_ref, e1_ref, s2_ref, e2_ref, s1_scr, s2_scr):
    K, H = PEER_N_KEYS, PEER_HEADS
    tm = x_ref.shape[0]
    h = x_ref[...] * (1.0 + sc_ref[...]) + sh_ref[...]
    hT = h.T.astype(MXU)
    ht_ref[...] = hT
    nb = tm // LANES
    halves = 2 if nb % 2 == 0 else 1
    hw = tm // halves

    def score_matmuls(j):
        cols = slice(j * hw, (j + 1) * hw)
        box = {}

        def query():
            box["q"] = jnp.dot(wq_ref[...], hT[:, cols], preferred_element_type=F32).astype(MXU)

        def first():
            s1_scr[:, :, cols] = jnp.dot(k1_ref[...], box["q"], preferred_element_type=F32).reshape(K, H, hw)

        def second():
            s2_scr[:, :, cols] = jnp.dot(k2_ref[...], box["q"], preferred_element_type=F32).reshape(K, H, hw)

        def second_by_head():
            s2_ref[:, :, cols] = jnp.dot(k2h_ref[...], box["q"], preferred_element_type=F32).reshape(H, K, hw)

        return [query, first, second, second_by_head]

    def lane_block(c):
        ls = slice(c * LANES, (c + 1) * LANES)
        s1 = s1_scr[:, :, ls]
        v1 = _top16([s1[k] for k in range(K)])
        v2 = _top16([s2_scr[k, :, ls] for k in range(K)])
        pairs = [(a, b) for a in range(PEER_TOPK) for b in range(PEER_TOPK)
                 if (a + 1) * (b + 1) <= PEER_TOPK]
        cand = [v1[a] + v2[b] for a, b in pairs]
        pad = [jnp.full_like(cand[0], -jnp.inf)] * (-len(cand) % PEER_TOPK)
        t = _top16(cand + pad)[-1]
        ex1 = [jnp.exp(v - v1[0]) for v in v1]
        ex2 = [jnp.exp(v - v2[0]) for v in v2]
        z = jnp.zeros_like(t)
        for (a, b), cv in zip(pairs, cand):
            z = z + jnp.where(cv >= t, ex1[a] * ex2[b], 0.0)
        inv_z = 1.0 / z
        thr = jnp.full(s1.shape, jnp.inf, F32)
        for b in range(PEER_TOPK):
            thr = jnp.where(s1 + v2[b][None] >= t[None], v2[b][None], thr)
        thr_ref[:, :, ls] = thr
        e1_ref[:, :, ls] = jnp.where(s1 >= v1[-1][None], jnp.exp(s1 - v1[0][None]), 0.0)
        for hh in range(H):
            s2h = s2_ref[hh, :, ls]
            e2_ref[hh, :, ls] = jnp.where(
                s2h >= v2[-1][hh:hh + 1], jnp.exp(s2h - v2[0][hh:hh + 1]) * inv_z[hh:hh + 1], 0.0)

    for piece in score_matmuls(0):
        piece()
    if halves == 1:
        for c in range(nb):
            lane_block(c)
    else:
        later = score_matmuls(1)
        first_blocks = list(range(nb // 2))
        per = -(-len(later) // len(first_blocks))
        for i, c in enumerate(first_blocks):
            for piece in later[i * per:(i + 1) * per]:
                piece()
            lane_block(c)
        for piece in later[len(first_blocks) * per:]:
            piece()
        for c in range(nb // 2, nb):
            lane_block(c)


def _router(x1, mod3, wts, row_of_tile, tm):
    n, D = x1.shape
    K, H = PEER_N_KEYS, PEER_HEADS
    HQ = H * PEER_DQ

    def const(a):
        return pl.BlockSpec(a.shape, lambda i: (0,) * a.ndim)

    kh = pl.BlockSpec((K, H, tm), lambda i: (0, 0, i))
    hk = pl.BlockSpec((H, K, tm), lambda i: (0, 0, i))
    return pl.pallas_call(
        _router_kernel,
        name="router",
        grid=(n // tm,),
        in_specs=[pl.BlockSpec((tm, D), lambda i: (i, 0)),
                  pl.BlockSpec((None, 1, D), lambda i: (row_of_tile(i, tm), 0, 3)),
                  pl.BlockSpec((None, 1, D), lambda i: (row_of_tile(i, tm), 0, 4)),
                  const(wts["wqT"]), const(wts["k1kh"]), const(wts["k2kh"]), const(wts["k2hk"])],
        out_specs=[pl.BlockSpec((D, tm), lambda i: (0, i)), kh, kh, hk, hk],
        out_shape=[jax.ShapeDtypeStruct((D, n), MXU),
                   jax.ShapeDtypeStruct((K, H, n), F32), jax.ShapeDtypeStruct((K, H, n), F32),
                   jax.ShapeDtypeStruct((H, K, n), F32), jax.ShapeDtypeStruct((H, K, n), F32)],
        scratch_shapes=[pltpu.VMEM((K, H, tm), F32), pltpu.VMEM((K, H, tm), F32)],
        compiler_params=_params(("parallel",), 58),
    )(x1, mod3, mod3, wts["wqT"], wts["k1kh"], wts["k2kh"], wts["k2hk"])


PEER_ROWS = 32
PEER_TE = 512


def _peer_kernel(go_ref, ua_ref, ub_ref, u0_ref, vta_ref, vtb_ref, vtl_ref, ht_ref,
                 thr_ref, e1_ref, s2_ref, e2_ref, x_ref, g2_ref, lg_ref, lb_ref, o_ref,
                 acc, sc0, sc1, a0, a1, *, alpha):
    e = pl.program_id(1)
    last = pl.num_programs(1) - 1
    te, tm = sc0.shape
    K = PEER_N_KEYS
    nsub = te // K
    go1, go2 = go_ref[0] != 0, go_ref[1] != 0

    nval = nsub
    nsc = nsub
    dr = acc.shape[0] // nval

    def scores(u_tile_ref, sc):
        def piece(i):
            rows = slice(i * (te // nsc), (i + 1) * (te // nsc))
            sc[rows, :] = jnp.dot(u_tile_ref[rows, :], ht_ref[...], preferred_element_type=F32)
        return [functools.partial(piece, i) for i in range(nsc)]

    def values(vt_tile_ref, a):
        def piece(r):
            rows = slice(r * dr, (r + 1) * dr)
            acc[rows, :] += jnp.dot(vt_tile_ref[rows, :], a[...], preferred_element_type=F32)
        return [functools.partial(piece, r) for r in range(nval)]

    def gates(sc, a, tile):
        def piece(c, rb):
            ls = slice(c * LANES, (c + 1) * LANES)
            rs = slice(rb * PEER_ROWS, (rb + 1) * PEER_ROWS)
            thr = [thr_ref[tile * nsub + sub, :, ls] for sub in range(nsub)]
            e1r = [e1_ref[tile * nsub + sub, :, ls] for sub in range(nsub)]
            g = [None] * nsub
            for h in range(PEER_HEADS):
                s2p, e2p = s2_ref[h, rs, ls], e2_ref[h, rs, ls]
                for sub in range(nsub):
                    w = jnp.where(s2p >= thr[sub][h:h + 1], e2p, 0.0) * e1r[sub][h:h + 1]
                    g[sub] = w if g[sub] is None else g[sub] + w
            for sub in range(nsub):
                xs = slice(sub * K + rb * PEER_ROWS, sub * K + (rb + 1) * PEER_ROWS)
                x = sc[xs, ls]
                act = 0.5 * x * (1.0 + lax.erf(x * SQRT_HALF))
                a[xs, ls] = (act * g[sub]).astype(a.dtype)
        return [functools.partial(piece, c, rb) for c in range(tm // LANES) for rb in range(K // PEER_ROWS)]

    def interleave(matmul_pieces, vector_pieces):
        per = -(-len(vector_pieces) // len(matmul_pieces))
        for i, mm in enumerate(matmul_pieces):
            mm()
            for vp in vector_pieces[i * per:(i + 1) * per]:
                vp()

    def mix(xs, ys):
        if len(xs) < len(ys):
            xs, ys = ys, xs
        step = len(xs) // len(ys)
        out = []
        for i, x in enumerate(xs):
            out.append(x)
            if i % step == step - 1 and i // step < len(ys):
                out.append(ys[i // step])
        return out

    @pl.when(e == 0)
    def _():
        acc[...] = jnp.zeros_like(acc)
        a1[...] = jnp.zeros_like(a1)
        for p in scores(u0_ref, sc0):
            p()

    @pl.when(go1)
    def _():
        interleave(mix(scores(ub_ref, sc1), values(vtb_ref, a1)), gates(sc0, a0, 2 * e))

    @pl.when(go2)
    def _():
        interleave(mix(scores(ua_ref, sc0), values(vta_ref, a0)), gates(sc1, a1, 2 * e + 1))

    @pl.when(e == last)
    def _():
        for p in values(vtl_ref, a1):
            p()
        y = alpha * x_ref[...] + g2_ref[...] * acc[...].T
        o_ref[...] = _layer_norm(y, lg_ref[...], lb_ref[...])


def _peer(x1, routed, u, vt, mod3, lg, lb, row_of_tile, alpha, tm):
    n, D = x1.shape
    K, H = PEER_N_KEYS, PEER_HEADS
    te = PEER_TE
    nt = u.shape[0] // te
    ht, thr, e1, s2, e2 = routed
    once = dict(pipeline_mode=pl.Buffered(1))
    kh = pl.BlockSpec((K, H, tm), lambda i, e: (0, 0, i), **once)
    hk = pl.BlockSpec((H, K, tm), lambda i, e: (0, 0, i), **once)
    vec = pl.BlockSpec((1, D), lambda i, e: (0, 0))
    return pl.pallas_call(
        functools.partial(_peer_kernel, alpha=alpha),
        name="peer",
        grid=(n // tm, nt // 2),
        in_specs=[pl.BlockSpec(memory_space=pltpu.SMEM),
                  pl.BlockSpec((te, D), lambda i, e: (jnp.minimum(2 * e + 2, nt - 1), 0)),
                  pl.BlockSpec((te, D), lambda i, e: (2 * e + 1, 0)),
                  pl.BlockSpec((te, D), lambda i, e: (0, 0), **once),
                  pl.BlockSpec((D, te), lambda i, e: (0, 2 * e)),
                  pl.BlockSpec((D, te), lambda i, e: (0, jnp.maximum(2 * e - 1, 0))),
                  pl.BlockSpec((D, te), lambda i, e: (0, nt - 1), **once),
                  pl.BlockSpec((D, tm), lambda i, e: (0, i), **once),
                  kh, kh, hk, hk,
                  pl.BlockSpec((tm, D), lambda i, e: (i, 0), **once),
                  pl.BlockSpec((None, 1, D), lambda i, e: (row_of_tile(i, tm), 0, 5)),
                  vec, vec],
        out_specs=pl.BlockSpec((tm, D), lambda i, e: (i, 0)),
        out_shape=jax.ShapeDtypeStruct((n, D), F32),
        scratch_shapes=[pltpu.VMEM((D, tm), F32), pltpu.VMEM((te, tm), F32), pltpu.VMEM((te, tm), F32),
                        pltpu.VMEM((te, tm), MXU), pltpu.VMEM((te, tm), MXU)],
        compiler_params=_params(("arbitrary", "arbitrary"), 58),
    )(jnp.ones((2,), jnp.int32), u, u, u, vt, vt, vt, ht, thr, e1, s2, e2, x1, mod3, lg, lb)


def _pad_heads(w, heads, width, lo=0):
    lead = w.shape[:-1]
    w = w.reshape(*lead, heads, width)
    w = jnp.pad(w, [(0, 0)] * len(lead) + [(0, 0), (lo, LANES - lo - width)])
    return w.reshape(*lead, heads * LANES)


def _layer_weights(l, seg, total, w_in, mla_q_norm, mla_w_qup, mla_kv_norm, mla_w_kvup, gqa_q_norm,
                   gqa_k_norm, w_br_a, w_br_b, w_br_c, w_out, peer_w_q, peer_k1, peer_k2, peer_u, peer_v):
    D = w_in.shape[1]
    widths = (MLA_Q_LORA, MLA_KV_LORA, MLA_ROPE, RET_HEADS * RET_DK, RET_HEADS * RET_DK,
              RET_HEADS * RET_DV, RET_HEADS * RET_DV, GQA_HEADS * GQA_DH, GQA_KV_HEADS * GQA_DH,
              GQA_KV_HEADS * GQA_DH, D, D, D)
    names = ("cq", "ckv", "kr", "rq", "rk", "rv", "rg", "gq", "gk", "gv", "ga", "gb", "gc")
    cols = dict(zip(names, jnp.split(w_in[l], np.cumsum(widths)[:-1].tolist(), axis=1)))
    lo = MLA_NOPE
    cols["kr"] = jnp.pad(cols["kr"], ((0, 0), (lo, LANES - lo - MLA_ROPE)))
    order = sorted(seg, key=lambda name: seg[name][0])
    used = seg[order[-1]][0] + seg[order[-1]][1]
    win = jnp.concatenate([cols[name] for name in order] + [jnp.zeros((D, total - used), F32)],
                          axis=1).astype(MXU)
    kv = mla_w_kvup[l].reshape(MLA_KV_LORA, MLA_HEADS, MLA_NOPE + MLA_V)
    eye = jnp.eye(PEER_HEADS, dtype=F32)
    half = PEER_DQ // 2
    k1p = jnp.pad(peer_k1[l], ((0, 0), (0, half)))
    k2p = jnp.pad(peer_k2[l], ((0, 0), (half, 0)))
    HQ = PEER_HEADS * PEER_DQ
    return {
        "win": win,
        "qn": mla_q_norm[l][None], "kvn": mla_kv_norm[l][None],
        "wq": _pad_heads(mla_w_qup[l], MLA_HEADS, MLA_NOPE + MLA_ROPE).astype(MXU),
        "wk": _pad_heads(kv[..., :MLA_NOPE].reshape(MLA_KV_LORA, -1), MLA_HEADS, MLA_NOPE).astype(MXU),
        "wv": _pad_heads(kv[..., MLA_NOPE:].reshape(MLA_KV_LORA, -1), MLA_HEADS, MLA_V).astype(MXU),
        "gqn": gqa_q_norm[l][None], "gkn": gqa_k_norm[l][None],
        "wa": _pad_heads(w_br_a[l].T, MLA_HEADS, MLA_V).T.astype(MXU),
        "wb": w_br_b[l].astype(MXU), "wc": w_br_c[l].astype(MXU), "wo": w_out[l].astype(MXU),
        "wqT": peer_w_q[l].T.astype(MXU),
        "k1kh": jnp.einsum("kd,hg->khgd", k1p, eye).reshape(HQ, HQ).astype(MXU),
        "k2kh": jnp.einsum("kd,hg->khgd", k2p, eye).reshape(HQ, HQ).astype(MXU),
        "k2hk": jnp.einsum("kd,hg->hkgd", k2p, eye).reshape(HQ, HQ).astype(MXU),
        "u": peer_u[l].astype(MXU), "vt": peer_v[l].T.astype(MXU),
    }


def _rope_tables(seq):
    t = jnp.arange(seq, dtype=jnp.int32)
    row, col = (t // GRID_W).astype(F32), (t % GRID_W).astype(F32)

    def tab(r):
        nf = r // 4
        inv = ROPE_BASE ** (-jnp.arange(nf, dtype=F32) / nf)
        ar, ac = row[:, None] * inv[None], col[:, None] * inv[None]
        cos = jnp.concatenate([jnp.cos(ar)] * 2 + [jnp.cos(ac)] * 2, axis=1)
        sin = jnp.concatenate([-jnp.sin(ar), jnp.sin(ar), -jnp.sin(ac), jnp.sin(ac)], axis=1)
        return cos, sin

    ca, sa = tab(MLA_ROPE)
    lo, hi = MLA_NOPE, LANES - MLA_NOPE - MLA_ROPE
    ca = jnp.pad(ca, ((0, 0), (lo, hi)), constant_values=1.0)
    sa = jnp.pad(sa, ((0, 0), (lo, hi)))
    cb, sb = tab(RET_DK)
    cb, sb = jnp.tile(cb, (1, RET_HEADS)), jnp.tile(sb, (1, RET_HEADS))
    cc, sc = tab(GQA_DH)
    return [ca, sa, cb, sb, cc, sc]


def kernel(x, c, ctx, c_ctx, w_mod, b_mod, w_in, mla_q_norm, mla_w_qup, mla_kv_norm, mla_w_kvup, ret_decay_logit, gqa_q_norm, gqa_k_norm, w_br_a, w_br_b, w_br_c, w_out, ln1_g, ln1_b, peer_w_q, peer_k1, peer_k2, peer_u, peer_v, ln2_g, ln2_b):
    B, S, D = x.shape
    CTX = ctx.shape[1]
    L = w_in.shape[0]
    alpha = float((2.0 * L) ** 0.25)
    seg, total = _layout(D)

    tm_l = _tile(S, 512)
    tm_c = _tile(B * CTX, 512)
    tpb = S // tm_l
    lat_row = lambda i, tm: (i * tm) // S
    ctx_row = lambda i, tm: B
    lat_tab = lambda i: i % tpb
    ctx_tab = lambda i: 0

    nrow = -(-(B + 1) // 8) * 8
    cvec = jnp.concatenate([c, c_ctx[None], jnp.zeros((nrow - B - 1, D), F32)], axis=0)
    mod = _modulation(cvec, w_mod, b_mod)

    tabs_l = _rope_tables(S)
    tabs_c = [jnp.ones((tm_c, a.shape[1]), F32) if k % 2 == 0 else jnp.zeros((tm_c, a.shape[1]), F32)
              for k, a in enumerate(tabs_l)]

    xl = x.reshape(B * S, D)
    xc = ctx.reshape(B * CTX, D)
    for l in range(L):
        last = l == L - 1
        W = _layer_weights(l, seg, total, w_in, mla_q_norm, mla_w_qup, mla_kv_norm, mla_w_kvup,
                           gqa_q_norm, gqa_k_norm, w_br_a, w_br_b, w_br_c, w_out,
                           peer_w_q, peer_k1, peer_k2, peer_u, peer_v)
        dec = _decay_tables(ret_decay_logit[l])
        mod3 = mod[l].reshape(nrow, 1, 6 * D)
        lg1, lb1, lg2, lb2 = ln1_g[l][None], ln1_b[l][None], ln2_g[l][None], ln2_b[l][None]

        Pl = _inproj(xl, mod3, W["win"], lat_row, _tile(S, 1024))
        Pc = _inproj(xc, mod3, W["win"], ctx_row, tm_c)
        qa, ka, va, qb, kb, qc, kc, vc = _prep(Pl, seg, W, tabs_l, lat_tab, tm_l)
        qa_c, ka_c, va_c, qb_c, kb_c, qc_c, kc_c, vc_c = _prep(Pc, seg, W, tabs_c, ctx_tab, tm_c)

        oa = _attention(qa, ka_c, va_c, ka, va, B, 1, S)
        oc = _attention(qc, kc_c, vc_c, kc, vc, B, GQA_GROUP, S)
        r0 = jnp.zeros((B, 2, RET_HEADS * RET_DK, RET_DV), F32)
        ob_c, r_ctx = _retention(qb_c, kb_c, Pc, seg, dec, r0, B)
        ob, _ = _retention(qb, kb, Pl, seg, dec, r_ctx, B)

        def tail(xs, oa, ob, oc, P, row_of_tile, tm):
            m = _merge(oa, ob, oc, P, seg, W["wa"], W["wb"], W["wc"], _tile(oa.shape[0], 1024))
            x1 = _outln(m, W["wo"], xs, mod3, lg1, lb1, row_of_tile, alpha, _tile(tm, 256))
            routed = _router(x1, mod3, W, row_of_tile, tm)
            return _peer(x1, routed, W["u"], W["vt"], mod3, lg2, lb2, row_of_tile, alpha, tm)

        if not last:
            oa_c = _attention(qa_c, ka_c, va_c, None, None, B, 1, CTX)
            oc_c = _attention(qc_c, kc_c, vc_c, None, None, B, GQA_GROUP, CTX)
            xc = tail(xc, oa_c, ob_c, oc_c, Pc, ctx_row, tm_c)
        xl = tail(xl, oa, ob, oc, Pl, lat_row, tm_l)
    return xl.reshape(B, S, D)
```
